```python
import jax, jax.numpy as jnp
from jax import lax
import numpy as np

D_MODEL = 1024
BATCH = 8
SEQ = 4096
DEPTH = 1
DEC_BATCH = 32
DEC_SEQ = 2048
PAST_LEN = 128

ATTN_WIDTH = D_MODEL // 2
CONV_WIDTH = D_MODEL - ATTN_WIDTH
HEAD_DIM = 64
N_HEADS = ATTN_WIDTH // HEAD_DIM
CONV_GROUPS = 8
CONV_K = 3
ROT_DIM = HEAD_DIM // 4
ROPE_THETA = 500000.0
DILATED_PAIRS = ((128, 1), (512, 4), (2048, 16))
PLE_DIM = 256
EPS = 1e-6
NEG = -1e30
IN_COLS = 4 * ATTN_WIDTH + 4 * CONV_WIDTH

kernel_name = "hymba_dilated_attn_shortconv_encoder"


def _rmsnorm(x, g):
    xf = x.astype(jnp.float32)
    y = xf * lax.rsqrt(jnp.mean(xf * xf, axis=-1, keepdims=True) + EPS)
    return (y * g.astype(jnp.float32)).astype(x.dtype)


def _rope_partial(t, pos):
    half = ROT_DIM // 2
    inv = jnp.power(jnp.float32(ROPE_THETA), -jnp.arange(0, ROT_DIM, 2, dtype=jnp.float32) / ROT_DIM)
    ang = pos.astype(jnp.float32)[:, None] * inv[None, :]
    cos = jnp.cos(ang)[None, :, None, :]
    sin = jnp.sin(ang)[None, :, None, :]
    r1 = t[..., :half]
    r2 = t[..., half:ROT_DIM]
    return jnp.concatenate([r1 * cos - r2 * sin, r2 * cos + r1 * sin, t[..., ROT_DIM:]], axis=-1)


def _window_stats(q, k, v, half):
    N, L, H, Dh = q.shape
    blk = half
    nb = -(-L // blk)
    Lp = nb * blk
    qb = jnp.pad(q, ((0, 0), (0, Lp - L), (0, 0), (0, 0))).reshape(N, nb, blk, H, Dh)
    kvpad = ((0, 0), (blk, Lp - L + blk), (0, 0), (0, 0))
    kb = jnp.pad(k, kvpad).reshape(N, nb + 2, blk, H, Dh)
    vb = jnp.pad(v, kvpad).reshape(N, nb + 2, blk, H, Dh)
    kw = jnp.concatenate([kb[:, :-2], kb[:, 1:-1], kb[:, 2:]], axis=2)
    vw = jnp.concatenate([vb[:, :-2], vb[:, 1:-1], vb[:, 2:]], axis=2)
    s = jnp.einsum('nbqhd,nbkhd->nbhqk', qb, kw) * (HEAD_DIM ** -0.5)
    qi = jnp.arange(nb)[:, None] * blk + jnp.arange(blk)[None, :]
    kj = (jnp.arange(nb)[:, None] - 1) * blk + jnp.arange(3 * blk)[None, :]
    mask = ((jnp.abs(qi[:, :, None] - kj[:, None, :]) <= half)
            & (kj[:, None, :] >= 0) & (kj[:, None, :] < L))
    s = jnp.where(mask[None, :, None], s, NEG)
    m = jnp.max(s, axis=-1)
    pr = jnp.exp(s - m[..., None])
    l = jnp.sum(pr, axis=-1)
    o = jnp.einsum('nbhqk,nbkhd->nbqhd', pr, vw).reshape(N, Lp, H, Dh)[:, :L]
    m = m.transpose(0, 1, 3, 2).reshape(N, Lp, H)[:, :L]
    l = l.transpose(0, 1, 3, 2).reshape(N, Lp, H)[:, :L]
    return o, m, l


def _to_classes(t, d):
    B, S = t.shape[:2]
    rest = t.shape[2:]
    t = t.reshape((B, S // d, d) + rest)
    t = jnp.swapaxes(t, 1, 2)
    return t.reshape((B * d, S // d) + rest)


def _from_classes(t, d, B):
    N, L = t.shape[:2]
    rest = t.shape[2:]
    t = t.reshape((B, d, L) + rest)
    t = jnp.swapaxes(t, 1, 2)
    return t.reshape((B, L * d) + rest)


def _dilated_attention(q, k, v):
    B = q.shape[0]
    outs, maxs, dens = [], [], []
    for window, dil in DILATED_PAIRS:
        half = window // (2 * dil)
        o, m, l = _window_stats(_to_classes(q, dil), _to_classes(k, dil), _to_classes(v, dil), half)
        outs.append(_from_classes(o, dil, B))
        maxs.append(_from_classes(m, dil, B))
        dens.append(_from_classes(l, dil, B))
    m_all = jnp.maximum(jnp.maximum(maxs[0], maxs[1]), maxs[2])
    ws = [jnp.exp(m_g - m_all) for m_g in maxs]
    num = ws[0][..., None] * outs[0] + ws[1][..., None] * outs[1] + ws[2][..., None] * outs[2]
    den = ws[0] * dens[0] + ws[1] * dens[1] + ws[2] * dens[2]
    return num / den[..., None]


def _short_conv(u, w):
    up = jnp.pad(u, ((0, 0), (1, 1), (0, 0)))
    return w[0] * up[:, :-2] + w[1] * up[:, 1:-1] + w[2] * up[:, 2:]


def _layer(x, p, norm_mix, w_in, conv_w, norm_attn_out, norm_conv_out, w_out,
           norm_ple, w_ple_gate, b_ple_gate, w_ple_proj):
    B, S, _ = x.shape
    pos = jnp.arange(S)
    h = _rmsnorm(x, norm_mix)
    proj = h @ w_in
    A, C = ATTN_WIDTH, CONV_WIDTH
    cuts = [A, 2 * A, 3 * A, 4 * A, 4 * A + C, 4 * A + 2 * C, 4 * A + 3 * C]
    q, k, v, g_a, c_b, c_c, c_h, g_c = jnp.split(proj, cuts, axis=-1)
    q = _rope_partial(q.reshape(B, S, N_HEADS, HEAD_DIM).astype(jnp.float32), pos)
    k = _rope_partial(k.reshape(B, S, N_HEADS, HEAD_DIM).astype(jnp.float32), pos)
    v = v.reshape(B, S, N_HEADS, HEAD_DIM).astype(jnp.float32)
    attn = _dilated_attention(q, k, v).reshape(B, S, A).astype(x.dtype)
    conv = c_b * _short_conv(c_c * c_h, conv_w)
    merged = jnp.concatenate([_rmsnorm(attn, norm_attn_out) * jax.nn.silu(g_a),
                              _rmsnorm(conv, norm_conv_out) * jax.nn.silu(g_c)], axis=-1)
    x = x + merged @ w_out
    gate = jax.nn.sigmoid(_rmsnorm(x, norm_ple) @ w_ple_gate + b_ple_gate)
    x = x + (p @ w_ple_proj) * gate
    return x


def setup_inputs(seed: int = 0) -> dict:
    key = jax.random.key(seed)
    ks = jax.random.split(key, 16)
    f32 = jnp.float32
    nrm = lambda k, shape, s: jax.random.normal(k, shape, f32) * s
    return {
        "x_prompt": nrm(ks[0], (BATCH, SEQ, D_MODEL), 1.0),
        "x_sample": nrm(ks[1], (DEC_BATCH, DEC_SEQ, D_MODEL), 1.0),
        "p_prompt": nrm(ks[2], (DEPTH, BATCH, SEQ, PLE_DIM), 1.0),
        "p_sample": nrm(ks[3], (DEPTH, DEC_BATCH, DEC_SEQ, PLE_DIM), 1.0),
        "norm_mix": 1.0 + nrm(ks[4], (DEPTH, D_MODEL), 0.02),
        "w_in": nrm(ks[5], (DEPTH, D_MODEL, IN_COLS), D_MODEL ** -0.5),
        "conv_w": nrm(ks[6], (DEPTH, CONV_K, CONV_WIDTH), CONV_K ** -0.5),
        "norm_attn_out": 1.0 + nrm(ks[7], (DEPTH, ATTN_WIDTH), 0.02),
        "norm_conv_out": 1.0 + nrm(ks[8], (DEPTH, CONV_WIDTH), 0.02),
        "w_out": nrm(ks[9], (DEPTH, D_MODEL, D_MODEL), D_MODEL ** -0.5),
        "norm_ple": 1.0 + nrm(ks[10], (DEPTH, D_MODEL), 0.02),
        "w_ple_gate": nrm(ks[11], (DEPTH, D_MODEL, D_MODEL), D_MODEL ** -0.5),
        "b_ple_gate": nrm(ks[12], (DEPTH, D_MODEL), 0.02),
        "w_ple_proj": nrm(ks[13], (DEPTH, PLE_DIM, D_MODEL), PLE_DIM ** -0.5),
        "norm_final": 1.0 + nrm(ks[14], (D_MODEL,), 0.02),
    }


def reference(x_prompt, x_sample, p_prompt, p_sample, norm_mix, w_in, conv_w,
              norm_attn_out, norm_conv_out, w_out, norm_ple, w_ple_gate, b_ple_gate,
              w_ple_proj, norm_final):
    def trunk(x, p):
        for i in range(DEPTH):
            x = _layer(x, p[i], norm_mix[i], w_in[i], conv_w[i], norm_attn_out[i],
                       norm_conv_out[i], w_out[i], norm_ple[i], w_ple_gate[i],
                       b_ple_gate[i], w_ple_proj[i])
        return _rmsnorm(x, norm_final)

    y_prompt = trunk(x_prompt, p_prompt)
    y_sample = trunk(x_sample, p_sample)
    return (y_prompt, y_sample)
```

```python
import functools

import jax
import jax.numpy as jnp
from jax import lax
from jax.experimental import pallas as pl
from jax.experimental.pallas import tpu as pltpu

D_MODEL = 1024
ATTN_WIDTH = 512
CONV_WIDTH = 512
HEAD_DIM = 64
ROT_DIM = 16
ROPE_THETA = 500000.0
DILATED_PAIRS = ((128, 1), (512, 4), (2048, 16))
PLE_DIM = 256
EPS = 1e-6
NEG = -1e30

LANES = 128
BF16_SUBLANES = 16
Q_BLOCK = 128
VMEM_LIMIT_BYTES = 56 * 1024 * 1024

_TRANS_B = (((1,), (1,)), ((), ()))


def _rms_scale(v):
    return lax.rsqrt(jnp.mean(v * v, axis=-1, keepdims=True) + EPS)


def _in_proj_kernel(x_ref, g_ref, w_ref, cos_ref, sinlo_ref, sinhi_ref,
                    q_ref, k_ref, v_ref, sga_ref, cb_ref, u_ref, sgc_ref):
    x = x_ref[...]
    h = (x * _rms_scale(x) * g_ref[...]).astype(jnp.bfloat16)

    def cols(j):
        return jnp.dot(h, w_ref[:, j * ATTN_WIDTH:(j + 1) * ATTN_WIDTH],
                       preferred_element_type=jnp.float32)

    cos = cos_ref[...]
    sinlo = sinlo_ref[...]
    sinhi = sinhi_ref[...]

    def rope(t):
        outs = []
        for c in range(ATTN_WIDTH // LANES):
            tc = t[:, c * LANES:(c + 1) * LANES]
            up = pltpu.roll(tc, LANES - ROT_DIM // 2, axis=1)
            dn = pltpu.roll(tc, ROT_DIM // 2, axis=1)
            outs.append(tc * cos + up * sinlo + dn * sinhi)
        return jnp.concatenate(outs, axis=1)

    q_ref[...] = (rope(cols(0)) * (HEAD_DIM ** -0.5)).astype(jnp.bfloat16)
    k_ref[...] = rope(cols(1)).astype(jnp.bfloat16)
    v_ref[...] = cols(2).astype(jnp.bfloat16)
    sga_ref[...] = jax.nn.silu(cols(3)).astype(jnp.bfloat16)
    cb_ref[...] = cols(4).astype(jnp.bfloat16)
    u_ref[...] = (cols(5) * cols(6)).astype(jnp.bfloat16)
    sgc_ref[...] = jax.nn.silu(cols(7)).astype(jnp.bfloat16)


def _in_proj(x2, g, w_bf16, cos_t, sinlo_t, sinhi_t, seq, tm):
    T = x2.shape[0]
    nseq = seq // tm
    row = lambda i: (i, 0)
    fixed = lambda i: (0, 0)
    pos = lambda i: (i % nseq, 0)
    out = jax.ShapeDtypeStruct((T, ATTN_WIDTH), jnp.bfloat16)
    return pl.pallas_call(
        _in_proj_kernel,
        grid=(T // tm,),
        in_specs=[
            pl.BlockSpec((tm, D_MODEL), row),
            pl.BlockSpec((1, D_MODEL), fixed),
            pl.BlockSpec(w_bf16.shape, fixed),
            pl.BlockSpec((tm, LANES), pos),
            pl.BlockSpec((tm, LANES), pos),
            pl.BlockSpec((tm, LANES), pos),
        ],
        out_specs=[pl.BlockSpec((tm, ATTN_WIDTH), row)] * 7,
        out_shape=[out] * 7,
        compiler_params=pltpu.CompilerParams(
            dimension_semantics=("arbitrary",),
            vmem_limit_bytes=VMEM_LIMIT_BYTES),
        name="in_proj",
    )(x2, g, w_bf16, cos_t, sinlo_t, sinhi_t)


def _band_attn_kernel(q_ref, k_ref, v_ref, o_ref, lse_ref, *, half, lq, win):
    L = k_ref.shape[0]
    i = pl.program_id(2)
    lane = lax.broadcasted_iota(jnp.int32, (1, LANES), 1)
    first_head = lane < HEAD_DIM
    rel = (lax.broadcasted_iota(jnp.int32, (Q_BLOCK, win), 0)
           - lax.broadcasted_iota(jnp.int32, (Q_BLOCK, win), 1))

    def q_block(qb, carry):
        r0 = pl.multiple_of(qb * Q_BLOCK, Q_BLOCK)
        q0 = i * lq + r0
        start = pl.multiple_of(jnp.clip(q0 - half, 0, L - win), BF16_SUBLANES)
        valid = jnp.abs(rel + (q0 - start)) <= half
        for c in range(ATTN_WIDTH // LANES):
            cs = slice(c * LANES, (c + 1) * LANES)
            qp = q_ref[pl.ds(r0, Q_BLOCK), cs]
            kw = k_ref[pl.ds(start, win), cs]
            vw = v_ref[pl.ds(start, win), cs]
            acc = None
            stats = []
            for sel in (first_head, jnp.logical_not(first_head)):
                s = lax.dot_general(jnp.where(sel, qp, 0), kw, _TRANS_B,
                                    preferred_element_type=jnp.float32)
                s = jnp.where(valid, s, NEG)
                m = jnp.max(s, axis=-1, keepdims=True)
                p = jnp.exp(s - m)
                l = jnp.sum(p, axis=-1, keepdims=True)
                pv = jnp.dot(p.astype(jnp.bfloat16), jnp.where(sel, vw, 0),
                             preferred_element_type=jnp.float32)
                acc = pv if acc is None else acc + pv
                stats.append((m, l))
            (m_a, l_a), (m_b, l_b) = stats
            inv = jnp.where(first_head, 1.0 / l_a, 1.0 / l_b)
            lse = jnp.where(first_head, m_a + jnp.log(l_a), m_b + jnp.log(l_b))
            o_ref[pl.ds(r0, Q_BLOCK), cs] = (acc * inv).astype(o_ref.dtype)
            lse_ref[pl.ds(r0, Q_BLOCK), cs] = lse
        return carry

    lax.fori_loop(0, lq // Q_BLOCK, q_block, 0)


def _band_attn(q, k, v, batch, seq, dil, half):
    L = seq // dil
    lq = min(L, 512)
    win = min(L, Q_BLOCK + 2 * half)
    view = lambda t: t.reshape(batch, L, dil * ATTN_WIDTH)
    q_spec = pl.BlockSpec((None, lq, ATTN_WIDTH), lambda b, r, i: (b, i, r))
    kv_spec = pl.BlockSpec((None, L, ATTN_WIDTH), lambda b, r, i: (b, 0, r))
    o, lse = pl.pallas_call(
        functools.partial(_band_attn_kernel, half=half, lq=lq, win=win),
        grid=(batch, dil, L // lq),
        in_specs=[q_spec, kv_spec, kv_spec],
        out_specs=[q_spec, q_spec],
        out_shape=[
            jax.ShapeDtypeStruct((batch, L, dil * ATTN_WIDTH), jnp.bfloat16),
            jax.ShapeDtypeStruct((batch, L, dil * ATTN_WIDTH), jnp.float32),
        ],
        compiler_params=pltpu.CompilerParams(
            dimension_semantics=("arbitrary", "arbitrary", "arbitrary"),
            vmem_limit_bytes=VMEM_LIMIT_BYTES),
        name=f"band_attn_d{dil}",
    )(view(q), view(k), view(v))
    return o.reshape(batch * seq, ATTN_WIDTH), lse.reshape(batch * seq, ATTN_WIDTH)


def _tail_kernel(x_ref, p_ref, o1_ref, o2_ref, o3_ref, l1_ref, l2_ref, l3_ref,
                 sga_ref, cb_ref, u_ref, uprev_ref, unext_ref, sgc_ref,
                 convw_ref, na_ref, nc_ref, wout_ref, nple_ref, wgate_ref,
                 bgate_ref, wple_ref, nfin_ref, y_ref, *, tiles_per_seq):
    tm = x_ref.shape[0]
    f32 = jnp.float32
    i = pl.program_id(0)

    l1, l2, l3 = l1_ref[...], l2_ref[...], l3_ref[...]
    m = jnp.maximum(jnp.maximum(l1, l2), l3)
    e1, e2, e3 = jnp.exp(l1 - m), jnp.exp(l2 - m), jnp.exp(l3 - m)
    num = (e1 * o1_ref[...].astype(f32) + e2 * o2_ref[...].astype(f32)
           + e3 * o3_ref[...].astype(f32))
    attn = num / (e1 + e2 + e3)

    u = u_ref[...].astype(f32)
    t_in_seq = i % tiles_per_seq
    prev_row = uprev_ref[BF16_SUBLANES - 1:BF16_SUBLANES, :].astype(f32)
    next_row = unext_ref[0:1, :].astype(f32)
    prev_row = jnp.where(t_in_seq == 0, 0.0, prev_row)
    next_row = jnp.where(t_in_seq == tiles_per_seq - 1, 0.0, next_row)
    ridx = lax.broadcasted_iota(jnp.int32, (tm, 1), 0)
    u_before = jnp.where(ridx == 0, prev_row, pltpu.roll(u, 1, axis=0))
    u_after = jnp.where(ridx == tm - 1, next_row, pltpu.roll(u, tm - 1, axis=0))
    cw = convw_ref[...]
    conv = cb_ref[...].astype(f32) * (cw[0:1] * u_before + cw[1:2] * u + cw[2:3] * u_after)

    merged = jnp.concatenate(
        [attn * _rms_scale(attn) * na_ref[...] * sga_ref[...].astype(f32),
         conv * _rms_scale(conv) * nc_ref[...] * sgc_ref[...].astype(f32)],
        axis=1).astype(jnp.bfloat16)
    x1 = x_ref[...] + jnp.dot(merged, wout_ref[...], preferred_element_type=f32)

    r = (x1 * _rms_scale(x1) * nple_ref[...]).astype(jnp.bfloat16)
    gate = jax.nn.sigmoid(jnp.dot(r, wgate_ref[...], preferred_element_type=f32)
                          + bgate_ref[...])
    emb = jnp.dot(p_ref[...].astype(jnp.bfloat16), wple_ref[...],
                  preferred_element_type=f32)
    x2 = x1 + emb * gate
    y_ref[...] = x2 * _rms_scale(x2) * nfin_ref[...]


def _tail(x2, p2, o_list, lse_list, sga, cb, u, sgc, conv_w, norm_attn, norm_conv,
          w_out, norm_ple, w_gate, b_gate, w_ple, norm_final, seq, tm):
    T = x2.shape[0]
    halo_blocks = T // BF16_SUBLANES
    per_tile = tm // BF16_SUBLANES
    row = lambda i: (i, 0)
    fixed = lambda i: (0, 0)
    prev = lambda i: (jnp.maximum(i * per_tile - 1, 0), 0)
    nxt = lambda i: (jnp.minimum((i + 1) * per_tile, halo_blocks - 1), 0)
    a_spec = pl.BlockSpec((tm, ATTN_WIDTH), row)
    halo = lambda f: pl.BlockSpec((BF16_SUBLANES, CONV_WIDTH), f)
    full = lambda a: pl.BlockSpec(a.shape, fixed)
    return pl.pallas_call(
        functools.partial(_tail_kernel, tiles_per_seq=seq // tm),
        grid=(T // tm,),
        in_specs=[
            pl.BlockSpec((tm, D_MODEL), row),
            pl.BlockSpec((tm, PLE_DIM), row),
            a_spec, a_spec, a_spec, a_spec, a_spec, a_spec,
            a_spec, a_spec, a_spec, halo(prev), halo(nxt), a_spec,
            full(conv_w), full(norm_attn), full(norm_conv), full(w_out),
            full(norm_ple), full(w_gate), full(b_gate), full(w_ple), full(norm_final),
        ],
        out_specs=pl.BlockSpec((tm, D_MODEL), row),
        out_shape=jax.ShapeDtypeStruct((T, D_MODEL), jnp.float32),
        compiler_params=pltpu.CompilerParams(
            dimension_semantics=("arbitrary",),
            vmem_limit_bytes=VMEM_LIMIT_BYTES),
        name="tail",
    )(x2, p2, *o_list, *lse_list, sga, cb, u, u, u, sgc, conv_w, norm_attn,
      norm_conv, w_out, norm_ple, w_gate, b_gate, w_ple, norm_final)


def _rope_tables(seq):
    half = ROT_DIM // 2
    inv = jnp.power(jnp.float32(ROPE_THETA),
                    -jnp.arange(0, ROT_DIM, 2, dtype=jnp.float32) / ROT_DIM)
    ang = jnp.arange(seq, dtype=jnp.float32)[:, None] * inv[None, :]
    cos, sin = jnp.cos(ang), jnp.sin(ang)
    pad = HEAD_DIM - ROT_DIM
    ones = jnp.ones((seq, pad), jnp.float32)
    zeros = jnp.zeros((seq, pad), jnp.float32)
    zh = jnp.zeros((seq, half), jnp.float32)
    per_head = lambda parts: jnp.tile(jnp.concatenate(parts, axis=1), (1, LANES // HEAD_DIM))
    return (per_head([cos, cos, ones]),
            per_head([-sin, zh, zeros]),
            per_head([zh, sin, zeros]))


def _trunk(x, p, tables, norm_mix, w_in, conv_w, norm_attn, norm_conv, w_out,
           norm_ple, w_gate, b_gate, w_ple, norm_final):
    B, S, _ = x.shape
    x2 = x.reshape(B * S, D_MODEL)
    p2 = p.reshape(B * S, PLE_DIM)
    cos_t, sinlo_t, sinhi_t = tables
    q, k, v, sga, cb, u, sgc = _in_proj(x2, norm_mix, w_in, cos_t[:S], sinlo_t[:S],
                                        sinhi_t[:S], S, tm=512)
    o_list, lse_list = [], []
    for window, dil in DILATED_PAIRS:
        o, lse = _band_attn(q, k, v, B, S, dil, window // (2 * dil))
        o_list.append(o)
        lse_list.append(lse)
    y = _tail(x2, p2, o_list, lse_list, sga, cb, u, sgc, conv_w, norm_attn,
              norm_conv, w_out, norm_ple, w_gate, b_gate, w_ple, norm_final, S, tm=256)
    return y.reshape(B, S, D_MODEL)


def kernel(x_prompt, x_sample, p_prompt, p_sample, norm_mix, w_in, conv_w,
           norm_attn_out, norm_conv_out, w_out, norm_ple, w_ple_gate, b_ple_gate,
           w_ple_proj, norm_final):
    depth = w_in.shape[0]
    assert depth == 1, "single-layer trunk"
    bf16 = jnp.bfloat16
    tables = _rope_tables(max(x_prompt.shape[1], x_sample.shape[1]))
    params = (norm_mix[0][None], w_in[0].astype(bf16), conv_w[0],
              norm_attn_out[0][None], norm_conv_out[0][None], w_out[0].astype(bf16),
              norm_ple[0][None], w_ple_gate[0].astype(bf16), b_ple_gate[0][None],
              w_ple_proj[0].astype(bf16), norm_final[None])
    y_prompt = _trunk(x_prompt, p_prompt[0], tables, *params)
    y_sample = _trunk(x_sample, p_sample[0], tables, *params)
    return (y_prompt, y_sample)
```

```python
import functools

import jax
import jax.numpy as jnp
from jax import lax
from jax.experimental import pallas as pl
from jax.experimental.pallas import tpu as pltpu

D_MODEL = 1024
ATTN_WIDTH = 512
CONV_WIDTH = 512
HEAD_DIM = 64
ROT_DIM = 16
ROPE_THETA = 500000.0
HALF_BAND = 64
DILATIONS = (1, 4, 16)
PLE_DIM = 256
EPS = 1e-6
NEG = -1e30
LOG2_E = 1.4426950408889634

LANES = 128
BF16_SUBLANES = 16
Q_BLOCK = 128
Q_BLOCKS_PER_STEP = 4
COPY_ROWS = 128
VMEM_LIMIT_BYTES = 56 * 1024 * 1024

_TRANS_B = (((1,), (1,)), ((), ()))


def _rms_scale(v):
    return lax.rsqrt(jnp.mean(v * v, axis=-1, keepdims=True) + EPS)


def _in_proj_kernel(x_ref, g_ref, w_ref, cos_ref, sinlo_ref, sinhi_ref,
                    q_ref, k_ref, v_ref, sga_ref, cb_ref, u_ref, sgc_ref):
    x = x_ref[...]
    h = (x * _rms_scale(x) * g_ref[...]).astype(jnp.bfloat16)

    def cols(j):
        return jnp.dot(h, w_ref[:, j * ATTN_WIDTH:(j + 1) * ATTN_WIDTH],
                       preferred_element_type=jnp.float32)

    cos = cos_ref[...]
    sinlo = sinlo_ref[...]
    sinhi = sinhi_ref[...]

    def rope(t):
        outs = []
        for c in range(ATTN_WIDTH // LANES):
            tc = t[:, c * LANES:(c + 1) * LANES]
            up = pltpu.roll(tc, LANES - ROT_DIM // 2, axis=1)
            dn = pltpu.roll(tc, ROT_DIM // 2, axis=1)
            outs.append(tc * cos + up * sinlo + dn * sinhi)
        return jnp.concatenate(outs, axis=1)

    first_head = lax.broadcasted_iota(jnp.int32, (1, LANES), 1) < HEAD_DIM

    def per_head(t, fill):
        outs = []
        for c in range(ATTN_WIDTH // LANES):
            tc = t[:, c * LANES:(c + 1) * LANES]
            outs += [jnp.where(first_head, tc, fill), jnp.where(first_head, fill, tc)]
        return jnp.concatenate(outs, axis=1).astype(jnp.bfloat16)

    q_ref[...] = per_head(rope(cols(0)) * (HEAD_DIM ** -0.5 * LOG2_E), 0.0)
    k_ref[...] = rope(cols(1)).astype(jnp.bfloat16)
    v_ref[...] = per_head(cols(2), 1.0)
    sga_ref[...] = jax.nn.silu(cols(3)).astype(jnp.bfloat16)
    cb_ref[...] = cols(4).astype(jnp.bfloat16)
    u_ref[...] = (cols(5) * cols(6)).astype(jnp.bfloat16)
    sgc_ref[...] = jax.nn.silu(cols(7)).astype(jnp.bfloat16)


def _in_proj(x2, g, w_bf16, cos_t, sinlo_t, sinhi_t, seq, tm):
    T = x2.shape[0]
    nseq = seq // tm
    row = lambda i: (i, 0)
    fixed = lambda i: (0, 0)
    pos = lambda i: (i % nseq, 0)
    out = jax.ShapeDtypeStruct((T, ATTN_WIDTH), jnp.bfloat16)
    out2 = jax.ShapeDtypeStruct((T, 2 * ATTN_WIDTH), jnp.bfloat16)
    spec = pl.BlockSpec((tm, ATTN_WIDTH), row)
    spec2 = pl.BlockSpec((tm, 2 * ATTN_WIDTH), row)
    return pl.pallas_call(
        _in_proj_kernel,
        grid=(T // tm,),
        in_specs=[
            pl.BlockSpec((tm, D_MODEL), row),
            pl.BlockSpec((1, D_MODEL), fixed),
            pl.BlockSpec(w_bf16.shape, fixed),
            pl.BlockSpec((tm, LANES), pos),
            pl.BlockSpec((tm, LANES), pos),
            pl.BlockSpec((tm, LANES), pos),
        ],
        out_specs=[spec2, spec, spec2, spec, spec, spec, spec],
        out_shape=[out2, out, out2, out, out, out, out],
        compiler_params=pltpu.CompilerParams(
            dimension_semantics=("arbitrary",),
            vmem_limit_bytes=VMEM_LIMIT_BYTES),
        name="in_proj",
    )(x2, g, w_bf16, cos_t, sinlo_t, sinhi_t)


def _dil_attn_kernel(q_ref, k_ref, v_ref, out_ref, fbuf, a16, b16, bias_scr, s_scr, p_scr,
                     *, seq):
    f32, bf16 = jnp.float32, jnp.bfloat16
    len4, len16 = seq // 4, seq // 16
    first_head = lax.broadcasted_iota(jnp.int32, (1, LANES), 1) < HEAD_DIM
    lo, hi = slice(0, LANES), slice(LANES, 2 * LANES)

    def head_variants(dst, rows, tq, tk, tv):
        dst[0, rows, :] = jnp.where(first_head, tq, 0.0).astype(bf16)
        dst[1, rows, :] = jnp.where(first_head, 0.0, tq).astype(bf16)
        dst[2, rows, :] = tk.astype(bf16)
        dst[3, rows, :] = jnp.where(first_head, tv, 1.0).astype(bf16)
        dst[4, rows, :] = jnp.where(first_head, 1.0, tv).astype(bf16)

    def to_f32(c, carry):
        rows = pl.ds(pl.multiple_of(c * COPY_ROWS, COPY_ROWS), COPY_ROWS)
        fbuf[0, rows, :] = q_ref[rows, lo].astype(f32) + q_ref[rows, hi].astype(f32)
        fbuf[1, rows, :] = k_ref[rows, :].astype(f32)
        fbuf[2, rows, :] = jnp.where(first_head, v_ref[rows, lo].astype(f32),
                                     v_ref[rows, hi].astype(f32))
        return carry

    def split4(c, carry):
        r = pl.multiple_of(c * COPY_ROWS, COPY_ROWS)
        src = pl.ds(4 * (r % len4) + r // len4, COPY_ROWS, stride=4)
        rows = pl.ds(r, COPY_ROWS)
        t = [fbuf[x, src, :] for x in range(3)]
        for x in range(3):
            fbuf[3 + x, rows, :] = t[x]
        head_variants(a16, rows, *t)
        return carry

    def split16(c, carry):
        r = pl.multiple_of(c * COPY_ROWS, COPY_ROWS)
        r16 = r // len16
        src = pl.ds((r16 % 4) * len4 + 4 * (r % len16) + r16 // 4, COPY_ROWS, stride=4)
        head_variants(b16, pl.ds(r, COPY_ROWS), *[fbuf[3 + x, src, :] for x in range(3)])
        return carry

    n_copy = seq // COPY_ROWS
    lax.fori_loop(0, n_copy, to_f32, 0)
    lax.fori_loop(0, n_copy, split4, 0)
    lax.fori_loop(0, n_copy, split16, 0)

    win_max = Q_BLOCK + 2 * HALF_BAND
    rel = (lax.broadcasted_iota(jnp.int32, (Q_BLOCK, win_max), 0)
           - lax.broadcasted_iota(jnp.int32, (Q_BLOCK, win_max), 1))
    for n in range(3):
        bias_scr[n] = jnp.where(jnp.abs(rel + n * HALF_BAND) <= HALF_BAND, 0.0, NEG)

    def run_group(src, dil, emit):
        cls_len = seq // dil
        win = min(cls_len, win_max)
        blocks_per_class = cls_len // Q_BLOCK

        def step(it, carry):
            info = []
            for jj in range(Q_BLOCKS_PER_STEP):
                t = it * Q_BLOCKS_PER_STEP + jj
                row = pl.multiple_of(t * Q_BLOCK, Q_BLOCK)
                cls = t // blocks_per_class
                q0 = (t % blocks_per_class) * Q_BLOCK
                w0 = jnp.clip(q0 - HALF_BAND, 0, cls_len - win)
                start = pl.multiple_of(cls * cls_len + w0, BF16_SUBLANES)
                bias = bias_scr[(q0 - w0) // HALF_BAND, :, :win]
                kw = src(2, pl.ds(start, win))
                for h in range(2):
                    s = lax.dot_general(src(h, pl.ds(row, Q_BLOCK)), kw, _TRANS_B,
                                        preferred_element_type=f32)
                    s_scr[2 * jj + h, :, :win] = s + bias
                info.append((row, cls, q0, start))
            maxes = []
            for idx in range(2 * Q_BLOCKS_PER_STEP):
                s = s_scr[idx, :, :win]
                m = jnp.max(s, axis=-1, keepdims=True)
                p_scr[idx, :, :win] = jnp.exp2(s - m).astype(bf16)
                maxes.append(m)
            for jj, (row, cls, q0, start) in enumerate(info):
                keys = pl.ds(start, win)
                r_a = jnp.dot(p_scr[2 * jj, :, :win], src(3, keys), preferred_element_type=f32)
                r_b = jnp.dot(p_scr[2 * jj + 1, :, :win], src(4, keys),
                              preferred_element_type=f32)
                o_un = jnp.where(first_head, r_a, r_b)
                den = pltpu.roll(jnp.where(first_head, r_b, r_a), HEAD_DIM, axis=1)
                m = jnp.where(first_head, maxes[2 * jj], maxes[2 * jj + 1])
                emit(row, cls, q0, o_un, m, den)
            return carry

        lax.fori_loop(0, seq // (Q_BLOCK * Q_BLOCKS_PER_STEP), step, 0)

    def emit_strided(base, dil):
        def emit(row, cls, q0, o_un, m, den):
            toks = pl.ds(cls + dil * q0, Q_BLOCK, stride=dil)
            fbuf[base, toks, :] = o_un
            fbuf[base + 1, toks, :] = m
            fbuf[base + 2, toks, :] = den
        return emit

    def emit_merged(row, cls, q0, o1, m1, den1):
        rows = pl.ds(row, Q_BLOCK)
        m16, m4 = fbuf[1, rows, :], fbuf[4, rows, :]
        m_all = jnp.maximum(jnp.maximum(m1, m4), m16)
        w1, w4, w16 = jnp.exp2(m1 - m_all), jnp.exp2(m4 - m_all), jnp.exp2(m16 - m_all)
        num = w1 * o1 + w4 * fbuf[3, rows, :] + w16 * fbuf[0, rows, :]
        den = w1 * den1 + w4 * fbuf[5, rows, :] + w16 * fbuf[2, rows, :]
        out_ref[rows, :] = (num / den).astype(out_ref.dtype)

    def token_order(n, rows):
        ref, cols = ((q_ref, lo), (q_ref, hi), (k_ref, lo), (v_ref, lo), (v_ref, hi))[n]
        return ref[rows, cols]

    run_group(lambda n, rows: b16[n, rows, :], 16, emit_strided(0, 16))
    run_group(lambda n, rows: a16[n, rows, :], 4, emit_strided(3, 4))
    run_group(token_order, 1, emit_merged)


def _dil_attn(q, k, v, batch, seq):
    spec = pl.BlockSpec((seq, LANES), lambda b, c: (b, c))
    spec2 = pl.BlockSpec((seq, 2 * LANES), lambda b, c: (b, c))
    n_tiles = 2 * Q_BLOCKS_PER_STEP
    win = Q_BLOCK + 2 * HALF_BAND
    return pl.pallas_call(
        functools.partial(_dil_attn_kernel, seq=seq),
        grid=(batch, ATTN_WIDTH // LANES),
        in_specs=[spec2, spec, spec2],
        out_specs=spec,
        out_shape=jax.ShapeDtypeStruct((batch * seq, ATTN_WIDTH), jnp.bfloat16),
        scratch_shapes=[
            pltpu.VMEM((6, seq, LANES), jnp.float32),
            pltpu.VMEM((5, seq, LANES), jnp.bfloat16),
            pltpu.VMEM((5, seq, LANES), jnp.bfloat16),
            pltpu.VMEM((3, Q_BLOCK, win), jnp.float32),
            pltpu.VMEM((n_tiles, Q_BLOCK, win), jnp.float32),
            pltpu.VMEM((n_tiles, Q_BLOCK, win), jnp.bfloat16),
        ],
        compiler_params=pltpu.CompilerParams(
            dimension_semantics=("arbitrary", "arbitrary"),
            vmem_limit_bytes=VMEM_LIMIT_BYTES),
        name="dil_attn",
    )(q, k, v)


def _tail_kernel(x_ref, p_ref, attn_ref, sga_ref, cb_ref, u_ref, uprev_ref, unext_ref,
                 sgc_ref, convw_ref, na_ref, nc_ref, wout_ref, nple_ref, wgate_ref,
                 bgate_ref, wple_ref, nfin_ref, y_ref, *, tiles_per_seq):
    tm = x_ref.shape[0]
    f32 = jnp.float32
    i = pl.program_id(0)
    attn = attn_ref[...].astype(f32)

    u = u_ref[...].astype(f32)
    t_in_seq = i % tiles_per_seq
    prev_row = uprev_ref[BF16_SUBLANES - 1:BF16_SUBLANES, :].astype(f32)
    next_row = unext_ref[0:1, :].astype(f32)
    prev_row = jnp.where(t_in_seq == 0, 0.0, prev_row)
    next_row = jnp.where(t_in_seq == tiles_per_seq - 1, 0.0, next_row)
    ridx = lax.broadcasted_iota(jnp.int32, (tm, 1), 0)
    u_before = jnp.where(ridx == 0, prev_row, pltpu.roll(u, 1, axis=0))
    u_after = jnp.where(ridx == tm - 1, next_row, pltpu.roll(u, tm - 1, axis=0))
    cw = convw_ref[...]
    conv = cb_ref[...].astype(f32) * (cw[0:1] * u_before + cw[1:2] * u + cw[2:3] * u_after)

    merged = jnp.concatenate(
        [attn * _rms_scale(attn) * na_ref[...] * sga_ref[...].astype(f32),
         conv * _rms_scale(conv) * nc_ref[...] * sgc_ref[...].astype(f32)],
        axis=1).astype(jnp.bfloat16)
    x1 = x_ref[...] + jnp.dot(merged, wout_ref[...], preferred_element_type=f32)

    r = (x1 * _rms_scale(x1) * nple_ref[...]).astype(jnp.bfloat16)
    gate = jax.nn.sigmoid(jnp.dot(r, wgate_ref[...], preferred_element_type=f32)
                          + bgate_ref[...])
    emb = jnp.dot(p_ref[...].astype(jnp.bfloat16), wple_ref[...],
                  preferred_element_type=f32)
    x2 = x1 + emb * gate
    y_ref[...] = x2 * _rms_scale(x2) * nfin_ref[...]


def _tail(x2, p2, attn, sga, cb, u, sgc, conv_w, norm_attn, norm_conv,
          w_out, norm_ple, w_gate, b_gate, w_ple, norm_final, seq, tm):
    T = x2.shape[0]
    halo_blocks = T // BF16_SUBLANES
    per_tile = tm // BF16_SUBLANES
    row = lambda i: (i, 0)
    fixed = lambda i: (0, 0)
    prev = lambda i: (jnp.maximum(i * per_tile - 1, 0), 0)
    nxt = lambda i: (jnp.minimum((i + 1) * per_tile, halo_blocks - 1), 0)
    a_spec = pl.BlockSpec((tm, ATTN_WIDTH), row)
    halo = lambda f: pl.BlockSpec((BF16_SUBLANES, CONV_WIDTH), f)
    full = lambda a: pl.BlockSpec(a.shape, fixed)
    return pl.pallas_call(
        functools.partial(_tail_kernel, tiles_per_seq=seq // tm),
        grid=(T // tm,),
        in_specs=[
            pl.BlockSpec((tm, D_MODEL), row),
            pl.BlockSpec((tm, PLE_DIM), row),
            a_spec, a_spec, a_spec, a_spec, halo(prev), halo(nxt), a_spec,
            full(conv_w), full(norm_attn), full(norm_conv), full(w_out),
            full(norm_ple), full(w_gate), full(b_gate), full(w_ple), full(norm_final),
        ],
        out_specs=pl.BlockSpec((tm, D_MODEL), row),
        out_shape=jax.ShapeDtypeStruct((T, D_MODEL), jnp.float32),
        compiler_params=pltpu.CompilerParams(
            dimension_semantics=("arbitrary",),
            vmem_limit_bytes=VMEM_LIMIT_BYTES),
        name="tail",
    )(x2, p2, attn, sga, cb, u, u, u, sgc, conv_w, norm_attn,
      norm_conv, w_out, norm_ple, w_gate, b_gate, w_ple, norm_final)


def _rope_tables(seq):
    half = ROT_DIM // 2
    inv = jnp.power(jnp.float32(ROPE_THETA),
                    -jnp.arange(0, ROT_DIM, 2, dtype=jnp.float32) / ROT_DIM)
    ang = jnp.arange(seq, dtype=jnp.float32)[:, None] * inv[None, :]
    cos, sin = jnp.cos(ang), jnp.sin(ang)
    pad = HEAD_DIM - ROT_DIM
    ones = jnp.ones((seq, pad), jnp.float32)
    zeros = jnp.zeros((seq, pad), jnp.float32)
    zh = jnp.zeros((seq, half), jnp.float32)
    per_head = lambda parts: jnp.tile(jnp.concatenate(parts, axis=1), (1, LANES // HEAD_DIM))
    return (per_head([cos, cos, ones]),
            per_head([-sin, zh, zeros]),
            per_head([zh, sin, zeros]))


def _trunk(x, p, tables, norm_mix, w_in, conv_w, norm_attn, norm_conv, w_out,
           norm_ple, w_gate, b_gate, w_ple, norm_final):
    B, S, _ = x.shape
    assert S % (16 * Q_BLOCK) == 0
    x2 = x.reshape(B * S, D_MODEL)
    p2 = p.reshape(B * S, PLE_DIM)
    cos_t, sinlo_t, sinhi_t = tables
    q, k, v, sga, cb, u, sgc = _in_proj(x2, norm_mix, w_in, cos_t[:S], sinlo_t[:S],
                                        sinhi_t[:S], S, tm=512)
    attn = _dil_attn(q, k, v, B, S)
    y = _tail(x2, p2, attn, sga, cb, u, sgc, conv_w, norm_attn,
              norm_conv, w_out, norm_ple, w_gate, b_gate, w_ple, norm_final, S, tm=256)
    return y.reshape(B, S, D_MODEL)


def kernel(x_prompt, x_sample, p_prompt, p_sample, norm_mix, w_in, conv_w,
           norm_attn_out, norm_conv_out, w_out, norm_ple, w_ple_gate, b_ple_gate,
           w_ple_proj, norm_final):
    depth = w_in.shape[0]
    assert depth == 1, "single-layer trunk"
    bf16 = jnp.bfloat16
    tables = _rope_tables(max(x_prompt.shape[1], x_sample.shape[1]))
    params = (norm_mix[0][None], w_in[0].astype(bf16), conv_w[0],
              norm_attn_out[0][None], norm_conv_out[0][None], w_out[0].astype(bf16),
              norm_ple[0][None], w_ple_gate[0].astype(bf16), b_ple_gate[0][None],
              w_ple_proj[0].astype(bf16), norm_final[None])
    y_prompt = _trunk(x_prompt, p_prompt[0], tables, *params)
    y_sample = _trunk(x_sample, p_sample[0], tables, *params)
    return (y_prompt, y_sample)
```

```python
import functools

import jax
import jax.numpy as jnp
from jax import lax
from jax.experimental import pallas as pl
from jax.experimental.pallas import tpu as pltpu

D_MODEL = 1024
ATTN_WIDTH = 512
CONV_WIDTH = 512
HEAD_DIM = 64
ROT_DIM = 16
ROPE_THETA = 500000.0
HALF_BAND = 64
DILATIONS = (1, 4, 16)
PLE_DIM = 256
EPS = 1e-6
NEG = -1e30
LOG2_E = 1.4426950408889634

LANES = 128
BF16_SUBLANES = 16
Q_BLOCK = 128
Q_BLOCKS_PER_STEP = 4
COPY_ROWS = 128
COPY_UNROLL = 4
VMEM_LIMIT_BYTES = 56 * 1024 * 1024

_TRANS_B = (((1,), (1,)), ((), ()))


def _rms_scale(v):
    return lax.rsqrt(jnp.mean(v * v, axis=-1, keepdims=True) + EPS)


def _in_proj_kernel(x_ref, g_ref, w_ref, cos_ref, sinlo_ref, sinhi_ref,
                    q_ref, k_ref, v_ref, sga_ref, cb_ref, u_ref, sgc_ref):
    x = x_ref[...]
    h = (x * _rms_scale(x) * g_ref[...]).astype(jnp.bfloat16)

    def cols(j):
        return jnp.dot(h, w_ref[:, j * ATTN_WIDTH:(j + 1) * ATTN_WIDTH],
                       preferred_element_type=jnp.float32)

    cos = cos_ref[...]
    sinlo = sinlo_ref[...]
    sinhi = sinhi_ref[...]

    def rope(t):
        outs = []
        for c in range(ATTN_WIDTH // LANES):
            tc = t[:, c * LANES:(c + 1) * LANES]
            up = pltpu.roll(tc, LANES - ROT_DIM // 2, axis=1)
            dn = pltpu.roll(tc, ROT_DIM // 2, axis=1)
            outs.append(tc * cos + up * sinlo + dn * sinhi)
        return jnp.concatenate(outs, axis=1)

    first_head = lax.broadcasted_iota(jnp.int32, (1, LANES), 1) < HEAD_DIM

    def per_head(t, fill):
        outs = []
        for c in range(ATTN_WIDTH // LANES):
            tc = t[:, c * LANES:(c + 1) * LANES]
            outs += [jnp.where(first_head, tc, fill), jnp.where(first_head, fill, tc)]
        return jnp.concatenate(outs, axis=1).astype(jnp.bfloat16)

    def words(t):
        return pltpu.bitcast(t, jnp.uint32)

    q_ref[...] = words(per_head(rope(cols(0)) * (HEAD_DIM ** -0.5 * LOG2_E), 0.0))
    k_ref[...] = words(rope(cols(1)).astype(jnp.bfloat16))
    v_ref[...] = words(per_head(cols(2), 1.0))
    sga_ref[...] = jax.nn.silu(cols(3)).astype(jnp.bfloat16)
    cb_ref[...] = cols(4).astype(jnp.bfloat16)
    u_ref[...] = (cols(5) * cols(6)).astype(jnp.bfloat16)
    sgc_ref[...] = jax.nn.silu(cols(7)).astype(jnp.bfloat16)


def _in_proj(x2, g, w_bf16, cos_t, sinlo_t, sinhi_t, seq, tm):
    T = x2.shape[0]
    nseq = seq // tm
    row = lambda i: (i, 0)
    fixed = lambda i: (0, 0)
    pos = lambda i: (i % nseq, 0)
    out = jax.ShapeDtypeStruct((T, ATTN_WIDTH), jnp.bfloat16)
    wout = jax.ShapeDtypeStruct((T // 2, ATTN_WIDTH), jnp.uint32)
    wout2 = jax.ShapeDtypeStruct((T // 2, 2 * ATTN_WIDTH), jnp.uint32)
    spec = pl.BlockSpec((tm, ATTN_WIDTH), row)
    wspec = pl.BlockSpec((tm // 2, ATTN_WIDTH), row)
    wspec2 = pl.BlockSpec((tm // 2, 2 * ATTN_WIDTH), row)
    return pl.pallas_call(
        _in_proj_kernel,
        grid=(T // tm,),
        in_specs=[
            pl.BlockSpec((tm, D_MODEL), row),
            pl.BlockSpec((1, D_MODEL), fixed),
            pl.BlockSpec(w_bf16.shape, fixed),
            pl.BlockSpec((tm, LANES), pos),
            pl.BlockSpec((tm, LANES), pos),
            pl.BlockSpec((tm, LANES), pos),
        ],
        out_specs=[wspec2, wspec, wspec2, spec, spec, spec, spec],
        out_shape=[wout2, wout, wout2, out, out, out, out],
        compiler_params=pltpu.CompilerParams(
            dimension_semantics=("arbitrary",),
            vmem_limit_bytes=VMEM_LIMIT_BYTES),
        name="in_proj",
    )(x2, g, w_bf16, cos_t, sinlo_t, sinhi_t)


def _dil_attn_kernel(qa_ref, qb_ref, k_ref, va_ref, vb_ref, out_ref, fbuf, a16, b16, bias_scr,
                     s_scr, p_scr, m_scr, *, seq):
    f32, bf16, u32 = jnp.float32, jnp.bfloat16, jnp.uint32
    len4, len16 = seq // 4, seq // 16
    first_head = lax.broadcasted_iota(jnp.int32, (1, LANES), 1) < HEAD_DIM
    inputs = (qa_ref, qb_ref, k_ref, va_ref, vb_ref)

    def rows_bf16(ref, lead, start, size):
        words = pl.ds(pl.multiple_of(start // 2, BF16_SUBLANES // 2), size // 2)
        return pltpu.bitcast(ref[(*lead, words, slice(None))], bf16)

    def split_row_pairs(words, lead, word0, dst, x, row_even, row_odd):
        w = words[(*lead, pl.ds(word0, COPY_ROWS, stride=2), slice(None))]
        even = lax.bitcast_convert_type(w << 16, f32).astype(bf16)
        odd = lax.bitcast_convert_type(w & u32(0xFFFF0000), f32).astype(bf16)
        half = COPY_ROWS // 2
        dst[x, pl.ds(pl.multiple_of(row_even // 2, half), half), :] = pltpu.bitcast(even, u32)
        dst[x, pl.ds(pl.multiple_of(row_odd // 2, half), half), :] = pltpu.bitcast(odd, u32)

    def split4(c, carry):
        chunks = len4 // COPY_ROWS
        pair = c // chunks
        u0 = (c % chunks) * COPY_ROWS
        for x in range(5):
            split_row_pairs(inputs[x], (), pair + 2 * u0, a16, x,
                            2 * pair * len4 + u0, (2 * pair + 1) * len4 + u0)
        return carry

    def split16(c, carry):
        chunks = len16 // COPY_ROWS
        w0 = (c % chunks) * COPY_ROWS
        pair = (c // chunks) % 2
        r4 = c // (2 * chunks)
        for x in range(5):
            split_row_pairs(a16, (x,), r4 * (len4 // 2) + pair + 2 * w0,
                            b16, x, (r4 + 8 * pair) * len16 + w0,
                            (r4 + 8 * pair + 4) * len16 + w0)
        return carry

    n_copy = seq // COPY_ROWS
    lax.fori_loop(0, n_copy // 2, split4, 0, unroll=COPY_UNROLL)
    lax.fori_loop(0, n_copy // 2, split16, 0, unroll=COPY_UNROLL)

    win_max = Q_BLOCK + 2 * HALF_BAND
    rel = (lax.broadcasted_iota(jnp.int32, (Q_BLOCK, win_max), 0)
           - lax.broadcasted_iota(jnp.int32, (Q_BLOCK, win_max), 1))
    for n in range(3):
        bias_scr[n] = jnp.where(jnp.abs(rel + n * HALF_BAND) <= HALF_BAND, 0.0, NEG)

    def run_group(src, dil, out_slot):
        cls_len = seq // dil
        win = min(cls_len, win_max)
        blocks_per_class = cls_len // Q_BLOCK

        n_steps = seq // (Q_BLOCK * Q_BLOCKS_PER_STEP)

        def blocks(step):
            for jj in range(Q_BLOCKS_PER_STEP):
                t = step * Q_BLOCKS_PER_STEP + jj
                row = pl.multiple_of(t * Q_BLOCK, Q_BLOCK)
                cls = t // blocks_per_class
                q0 = (t % blocks_per_class) * Q_BLOCK
                w0 = jnp.clip(q0 - HALF_BAND, 0, cls_len - win)
                start = pl.multiple_of(cls * cls_len + w0, BF16_SUBLANES)
                yield jj, row, cls, q0, w0, start

        def scores(step, par):
            for jj, row, cls, q0, w0, start in blocks(step):
                bias = bias_scr[(q0 - w0) // HALF_BAND, :, :win]
                kw = src(2, start, win)
                for h in range(2):
                    s = lax.dot_general(src(h, row, Q_BLOCK), kw, _TRANS_B,
                                        preferred_element_type=f32)
                    s_scr[par, 2 * jj + h, :, :win] = s + bias

        def softmax(par):
            for jj in range(Q_BLOCKS_PER_STEP):
                maxes = []
                for idx in (2 * jj, 2 * jj + 1):
                    s = s_scr[par, idx, :, :win]
                    m = jnp.max(s, axis=-1, keepdims=True)
                    p_scr[par, idx, :, :win] = jnp.exp2(s - m).astype(bf16)
                    maxes.append(m)
                m_scr[par, jj] = jnp.where(first_head, maxes[0], maxes[1])

        def values(step, par):
            for jj, row, cls, q0, w0, start in blocks(step):
                r_a = jnp.dot(p_scr[par, 2 * jj, :, :win], src(3, start, win),
                              preferred_element_type=f32)
                r_b = jnp.dot(p_scr[par, 2 * jj + 1, :, :win], src(4, start, win),
                              preferred_element_type=f32)
                o_un = jnp.where(first_head, r_a, r_b)
                den = pltpu.roll(jnp.where(first_head, r_b, r_a), HEAD_DIM, axis=1)
                toks = pl.ds(cls + dil * q0, Q_BLOCK, stride=dil) if dil > 1 else pl.ds(row, Q_BLOCK)
                fbuf[out_slot, toks, :] = o_un / den
                fbuf[out_slot + 1, toks, :] = m_scr[par, jj] + jnp.log2(den)

        assert n_steps % 2 == 0 and n_steps >= 2
        scores(0, 0)
        softmax(0)
        scores(1, 1)

        def body(j, carry):
            step = 2 * j
            values(step, 0)
            softmax(1)
            scores(step + 2, 0)
            values(step + 1, 1)
            softmax(0)
            scores(step + 3, 1)
            return carry

        lax.fori_loop(0, (n_steps - 2) // 2, body, 0)
        values(n_steps - 2, 0)
        softmax(1)
        values(n_steps - 1, 1)

    run_group(lambda n, start, size: rows_bf16(b16, (n,), start, size), 16, 0)
    run_group(lambda n, start, size: rows_bf16(a16, (n,), start, size), 4, 2)
    run_group(lambda n, start, size: rows_bf16(inputs[n], (), start, size), 1, 4)

    def merge(c, carry):
        rows = pl.ds(pl.multiple_of(c * COPY_ROWS, COPY_ROWS), COPY_ROWS)
        lse = [fbuf[2 * g + 1, rows, :] for g in range(3)]
        top = jnp.maximum(jnp.maximum(lse[0], lse[1]), lse[2])
        w = [jnp.exp2(l - top) for l in lse]
        num = w[0] * fbuf[0, rows, :] + w[1] * fbuf[2, rows, :] + w[2] * fbuf[4, rows, :]
        out_ref[rows, :] = (num / (w[0] + w[1] + w[2])).astype(out_ref.dtype)
        return carry

    lax.fori_loop(0, n_copy, merge, 0, unroll=COPY_UNROLL)


def _dil_attn(q, k, v, batch, seq):
    words = seq // 2
    pair = pl.BlockSpec((words, LANES), lambda b, c: (b, c))
    head_a = pl.BlockSpec((words, LANES), lambda b, c: (b, 2 * c))
    head_b = pl.BlockSpec((words, LANES), lambda b, c: (b, 2 * c + 1))
    n_tiles = 2 * Q_BLOCKS_PER_STEP
    win = Q_BLOCK + 2 * HALF_BAND
    return pl.pallas_call(
        functools.partial(_dil_attn_kernel, seq=seq),
        grid=(batch, ATTN_WIDTH // LANES),
        in_specs=[head_a, head_b, pair, head_a, head_b],
        out_specs=pl.BlockSpec((seq, LANES), lambda b, c: (b, c)),
        out_shape=jax.ShapeDtypeStruct((batch * seq, ATTN_WIDTH), jnp.bfloat16),
        scratch_shapes=[
            pltpu.VMEM((6, seq, LANES), jnp.float32),
            pltpu.VMEM((5, words, LANES), jnp.uint32),
            pltpu.VMEM((5, words, LANES), jnp.uint32),
            pltpu.VMEM((3, Q_BLOCK, win), jnp.float32),
            pltpu.VMEM((2, n_tiles, Q_BLOCK, win), jnp.float32),
            pltpu.VMEM((2, n_tiles, Q_BLOCK, win), jnp.bfloat16),
            pltpu.VMEM((2, Q_BLOCKS_PER_STEP, Q_BLOCK, LANES), jnp.float32),
        ],
        compiler_params=pltpu.CompilerParams(
            dimension_semantics=("arbitrary", "arbitrary"),
            vmem_limit_bytes=VMEM_LIMIT_BYTES),
        name="dil_attn",
    )(q, q, k, v, v)


def _tail_kernel(x_ref, p_ref, attn_ref, sga_ref, cb_ref, u_ref, uprev_ref, unext_ref,
                 sgc_ref, convw_ref, na_ref, nc_ref, wout_ref, nple_ref, wgate_ref,
                 bgate_ref, wple_ref, nfin_ref, y_ref, *, tiles_per_seq):
    tm = x_ref.shape[0]
    f32 = jnp.float32
    i = pl.program_id(0)
    attn = attn_ref[...].astype(f32)

    u = u_ref[...].astype(f32)
    t_in_seq = i % tiles_per_seq
    prev_row = uprev_ref[BF16_SUBLANES - 1:BF16_SUBLANES, :].astype(f32)
    next_row = unext_ref[0:1, :].astype(f32)
    prev_row = jnp.where(t_in_seq == 0, 0.0, prev_row)
    next_row = jnp.where(t_in_seq == tiles_per_seq - 1, 0.0, next_row)
    ridx = lax.broadcasted_iota(jnp.int32, (tm, 1), 0)
    u_before = jnp.where(ridx == 0, prev_row, pltpu.roll(u, 1, axis=0))
    u_after = jnp.where(ridx == tm - 1, next_row, pltpu.roll(u, tm - 1, axis=0))
    cw = convw_ref[...]
    conv = cb_ref[...].astype(f32) * (cw[0:1] * u_before + cw[1:2] * u + cw[2:3] * u_after)

    merged = jnp.concatenate(
        [attn * _rms_scale(attn) * na_ref[...] * sga_ref[...].astype(f32),
         conv * _rms_scale(conv) * nc_ref[...] * sgc_ref[...].astype(f32)],
        axis=1).astype(jnp.bfloat16)
    x1 = x_ref[...] + jnp.dot(merged, wout_ref[...], preferred_element_type=f32)

    r = (x1 * _rms_scale(x1) * nple_ref[...]).astype(jnp.bfloat16)
    gate = jax.nn.sigmoid(jnp.dot(r, wgate_ref[...], preferred_element_type=f32)
                          + bgate_ref[...])
    emb = jnp.dot(p_ref[...].astype(jnp.bfloat16), wple_ref[...],
                  preferred_element_type=f32)
    x2 = x1 + emb * gate
    y_ref[...] = x2 * _rms_scale(x2) * nfin_ref[...]


def _tail(x2, p2, attn, sga, cb, u, sgc, conv_w, norm_attn, norm_conv,
          w_out, norm_ple, w_gate, b_gate, w_ple, norm_final, seq, tm):
    T = x2.shape[0]
    halo_blocks = T // BF16_SUBLANES
    per_tile = tm // BF16_SUBLANES
    row = lambda i: (i, 0)
    fixed = lambda i: (0, 0)
    prev = lambda i: (jnp.maximum(i * per_tile - 1, 0), 0)
    nxt = lambda i: (jnp.minimum((i + 1) * per_tile, halo_blocks - 1), 0)
    a_spec = pl.BlockSpec((tm, ATTN_WIDTH), row)
    halo = lambda f: pl.BlockSpec((BF16_SUBLANES, CONV_WIDTH), f)
    full = lambda a: pl.BlockSpec(a.shape, fixed)
    return pl.pallas_call(
        functools.partial(_tail_kernel, tiles_per_seq=seq // tm),
        grid=(T // tm,),
        in_specs=[
            pl.BlockSpec((tm, D_MODEL), row),
            pl.BlockSpec((tm, PLE_DIM), row),
            a_spec, a_spec, a_spec, a_spec, halo(prev), halo(nxt), a_spec,
            full(conv_w), full(norm_attn), full(norm_conv), full(w_out),
            full(norm_ple), full(w_gate), full(b_gate), full(w_ple), full(norm_final),
        ],
        out_specs=pl.BlockSpec((tm, D_MODEL), row),
        out_shape=jax.ShapeDtypeStruct((T, D_MODEL), jnp.float32),
        compiler_params=pltpu.CompilerParams(
            dimension_semantics=("arbitrary",),
            vmem_limit_bytes=VMEM_LIMIT_BYTES),
        name="tail",
    )(x2, p2, attn, sga, cb, u, u, u, sgc, conv_w, norm_attn,
      norm_conv, w_out, norm_ple, w_gate, b_gate, w_ple, norm_final)


def _rope_tables(seq):
    half = ROT_DIM // 2
    inv = jnp.power(jnp.float32(ROPE_THETA),
                    -jnp.arange(0, ROT_DIM, 2, dtype=jnp.float32) / ROT_DIM)
    ang = jnp.arange(seq, dtype=jnp.float32)[:, None] * inv[None, :]
    cos, sin = jnp.cos(ang), jnp.sin(ang)
    pad = HEAD_DIM - ROT_DIM
    ones = jnp.ones((seq, pad), jnp.float32)
    zeros = jnp.zeros((seq, pad), jnp.float32)
    zh = jnp.zeros((seq, half), jnp.float32)
    per_head = lambda parts: jnp.tile(jnp.concatenate(parts, axis=1), (1, LANES // HEAD_DIM))
    return (per_head([cos, cos, ones]),
            per_head([-sin, zh, zeros]),
            per_head([zh, sin, zeros]))


def _trunk(x, p, tables, norm_mix, w_in, conv_w, norm_attn, norm_conv, w_out,
           norm_ple, w_gate, b_gate, w_ple, norm_final):
    B, S, _ = x.shape
    assert S % (16 * Q_BLOCK) == 0
    x2 = x.reshape(B * S, D_MODEL)
    p2 = p.reshape(B * S, PLE_DIM)
    cos_t, sinlo_t, sinhi_t = tables
    q, k, v, sga, cb, u, sgc = _in_proj(x2, norm_mix, w_in, cos_t[:S], sinlo_t[:S],
                                        sinhi_t[:S], S, tm=512)
    attn = _dil_attn(q, k, v, B, S)
    y = _tail(x2, p2, attn, sga, cb, u, sgc, conv_w, norm_attn,
              norm_conv, w_out, norm_ple, w_gate, b_gate, w_ple, norm_final, S, tm=256)
    return y.reshape(B, S, D_MODEL)


def kernel(x_prompt, x_sample, p_prompt, p_sample, norm_mix, w_in, conv_w,
           norm_attn_out, norm_conv_out, w_out, norm_ple, w_ple_gate, b_ple_gate,
           w_ple_proj, norm_final):
    depth = w_in.shape[0]
    assert depth == 1, "single-layer trunk"
    bf16 = jnp.bfloat16
    tables = _rope_tables(max(x_prompt.shape[1], x_sample.shape[1]))
    params = (norm_mix[0][None], w_in[0].astype(bf16), conv_w[0],
              norm_attn_out[0][None], norm_conv_out[0][None], w_out[0].astype(bf16),
              norm_ple[0][None], w_ple_gate[0].astype(bf16), b_ple_gate[0][None],
              w_ple_proj[0].astype(bf16), norm_final[None])
    y_prompt = _trunk(x_prompt, p_prompt[0], tables, *params)
    y_sample = _trunk(x_sample, p_sample[0], tables, *params)
    return (y_prompt, y_sample)
```

```python
import functools

import jax
import jax.numpy as jnp
from jax import lax
from jax.experimental import pallas as pl
from jax.experimental.pallas import tpu as pltpu

D_MODEL = 1024
ATTN_WIDTH = 512
CONV_WIDTH = 512
HEAD_DIM = 64
ROT_DIM = 16
ROPE_THETA = 500000.0
HALF_BAND = 64
DILATIONS = (1, 4, 16)
PLE_DIM = 256
EPS = 1e-6
NEG = -1e30
LOG2_E = 1.4426950408889634

LANES = 128
BF16_SUBLANES = 16
Q_BLOCK = 128
Q_BLOCKS_PER_STEP = 4
COPY_ROWS = 128
COPY_UNROLL = 4
VMEM_LIMIT_BYTES = 56 * 1024 * 1024

_TRANS_B = (((1,), (1,)), ((), ()))


def _rms_scale(v):
    return lax.rsqrt(jnp.mean(v * v, axis=-1, keepdims=True) + EPS)


def _in_proj_kernel(x_ref, g_ref, w_ref, cos_ref, sinlo_ref, sinhi_ref,
                    q_ref, k_ref, v_ref, sga_ref, cb_ref, u_ref, sgc_ref):
    x = x_ref[...]
    h = (x * _rms_scale(x) * g_ref[...]).astype(jnp.bfloat16)

    def cols(j):
        return jnp.dot(h, w_ref[:, j * ATTN_WIDTH:(j + 1) * ATTN_WIDTH],
                       preferred_element_type=jnp.float32)

    cos = cos_ref[...]
    sinlo = sinlo_ref[...]
    sinhi = sinhi_ref[...]

    def rope(t):
        outs = []
        for c in range(ATTN_WIDTH // LANES):
            tc = t[:, c * LANES:(c + 1) * LANES]
            up = pltpu.roll(tc, LANES - ROT_DIM // 2, axis=1)
            dn = pltpu.roll(tc, ROT_DIM // 2, axis=1)
            outs.append(tc * cos + up * sinlo + dn * sinhi)
        return jnp.concatenate(outs, axis=1)

    first_head = lax.broadcasted_iota(jnp.int32, (1, LANES), 1) < HEAD_DIM

    def per_head(t, fill):
        outs = []
        for c in range(ATTN_WIDTH // LANES):
            tc = t[:, c * LANES:(c + 1) * LANES]
            outs += [jnp.where(first_head, tc, fill), jnp.where(first_head, fill, tc)]
        return jnp.concatenate(outs, axis=1).astype(jnp.bfloat16)

    def words(t):
        return pltpu.bitcast(t, jnp.uint32)

    q_ref[...] = words(per_head(rope(cols(0)) * (HEAD_DIM ** -0.5 * LOG2_E), 0.0))
    k_ref[...] = words(rope(cols(1)).astype(jnp.bfloat16))
    v_ref[...] = words(per_head(cols(2), 1.0))
    sga_ref[...] = jax.nn.silu(cols(3)).astype(jnp.bfloat16)
    cb_ref[...] = cols(4).astype(jnp.bfloat16)
    u_ref[...] = (cols(5) * cols(6)).astype(jnp.bfloat16)
    sgc_ref[...] = jax.nn.silu(cols(7)).astype(jnp.bfloat16)


def _in_proj(x2, g, w_bf16, cos_t, sinlo_t, sinhi_t, seq, tm):
    T = x2.shape[0]
    nseq = seq // tm
    row = lambda i: (i, 0)
    fixed = lambda i: (0, 0)
    pos = lambda i: (i % nseq, 0)
    out = jax.ShapeDtypeStruct((T, ATTN_WIDTH), jnp.bfloat16)
    wout = jax.ShapeDtypeStruct((T // 2, ATTN_WIDTH), jnp.uint32)
    wout2 = jax.ShapeDtypeStruct((T // 2, 2 * ATTN_WIDTH), jnp.uint32)
    spec = pl.BlockSpec((tm, ATTN_WIDTH), row)
    wspec = pl.BlockSpec((tm // 2, ATTN_WIDTH), row)
    wspec2 = pl.BlockSpec((tm // 2, 2 * ATTN_WIDTH), row)
    return pl.pallas_call(
        _in_proj_kernel,
        grid=(T // tm,),
        in_specs=[
            pl.BlockSpec((tm, D_MODEL), row),
            pl.BlockSpec((1, D_MODEL), fixed),
            pl.BlockSpec(w_bf16.shape, fixed),
            pl.BlockSpec((tm, LANES), pos),
            pl.BlockSpec((tm, LANES), pos),
            pl.BlockSpec((tm, LANES), pos),
        ],
        out_specs=[wspec2, wspec, wspec2, spec, spec, spec, spec],
        out_shape=[wout2, wout, wout2, out, out, out, out],
        compiler_params=pltpu.CompilerParams(
            dimension_semantics=("arbitrary",),
            vmem_limit_bytes=VMEM_LIMIT_BYTES),
        name="in_proj",
    )(x2, g, w_bf16, cos_t, sinlo_t, sinhi_t)


def _dil_attn_kernel(qa_ref, qb_ref, k_ref, va_ref, vb_ref, out_ref, fbuf, a16, b16, bias_scr,
                     s_scr, p_scr, m_scr, *, seq):
    f32, bf16, u32 = jnp.float32, jnp.bfloat16, jnp.uint32
    len4, len16 = seq // 4, seq // 16
    first_head = lax.broadcasted_iota(jnp.int32, (1, LANES), 1) < HEAD_DIM
    inputs = (qa_ref, qb_ref, k_ref, va_ref, vb_ref)

    def rows_bf16(ref, lead, start, size):
        words = pl.ds(pl.multiple_of(start // 2, BF16_SUBLANES // 2), size // 2)
        return pltpu.bitcast(ref[(*lead, words, slice(None))], bf16)

    def split_row_pairs(words, lead, word0, dst, x, row_even, row_odd):
        w = words[(*lead, pl.ds(word0, COPY_ROWS, stride=2), slice(None))]
        even, odd = (pltpu.unpack_elementwise(w, index=i, packed_dtype=bf16,
                                              unpacked_dtype=f32).astype(bf16) for i in (0, 1))
        half = COPY_ROWS // 2
        dst[x, pl.ds(pl.multiple_of(row_even // 2, half), half), :] = pltpu.bitcast(even, u32)
        dst[x, pl.ds(pl.multiple_of(row_odd // 2, half), half), :] = pltpu.bitcast(odd, u32)

    def split4(c, carry):
        chunks = len4 // COPY_ROWS
        pair = c // chunks
        u0 = (c % chunks) * COPY_ROWS
        for x in range(5):
            split_row_pairs(inputs[x], (), pair + 2 * u0, a16, x,
                            2 * pair * len4 + u0, (2 * pair + 1) * len4 + u0)
        return carry

    def split16(c, carry):
        chunks = len16 // COPY_ROWS
        w0 = (c % chunks) * COPY_ROWS
        pair = (c // chunks) % 2
        r4 = c // (2 * chunks)
        for x in range(5):
            split_row_pairs(a16, (x,), r4 * (len4 // 2) + pair + 2 * w0,
                            b16, x, (r4 + 8 * pair) * len16 + w0,
                            (r4 + 8 * pair + 4) * len16 + w0)
        return carry

    n_copy = seq // COPY_ROWS
    lax.fori_loop(0, n_copy // 2, split4, 0, unroll=COPY_UNROLL)
    lax.fori_loop(0, n_copy // 2, split16, 0, unroll=COPY_UNROLL)

    win_max = Q_BLOCK + 2 * HALF_BAND
    rel = (lax.broadcasted_iota(jnp.int32, (Q_BLOCK, win_max), 0)
           - lax.broadcasted_iota(jnp.int32, (Q_BLOCK, win_max), 1))
    for n in range(3):
        bias_scr[n] = jnp.where(jnp.abs(rel + n * HALF_BAND) <= HALF_BAND, 0.0, NEG)

    def run_group(src, dil, out_slot):
        cls_len = seq // dil
        win = min(cls_len, win_max)
        blocks_per_class = cls_len // Q_BLOCK

        n_steps = seq // (Q_BLOCK * Q_BLOCKS_PER_STEP)

        def blocks(step):
            for jj in range(Q_BLOCKS_PER_STEP):
                t = step * Q_BLOCKS_PER_STEP + jj
                row = pl.multiple_of(t * Q_BLOCK, Q_BLOCK)
                cls = t // blocks_per_class
                q0 = (t % blocks_per_class) * Q_BLOCK
                w0 = jnp.clip(q0 - HALF_BAND, 0, cls_len - win)
                start = pl.multiple_of(cls * cls_len + w0, BF16_SUBLANES)
                yield jj, row, cls, q0, w0, start

        def scores(step, par):
            for jj, row, cls, q0, w0, start in blocks(step):
                bias = bias_scr[(q0 - w0) // HALF_BAND, :, :win]
                kw = src(2, start, win)
                for h in range(2):
                    s = lax.dot_general(src(h, row, Q_BLOCK), kw, _TRANS_B,
                                        preferred_element_type=f32)
                    s_scr[par, 2 * jj + h, :, :win] = s + bias

        def softmax(par):
            for jj in range(Q_BLOCKS_PER_STEP):
                maxes = []
                for idx in (2 * jj, 2 * jj + 1):
                    s = s_scr[par, idx, :, :win]
                    m = jnp.max(s, axis=-1, keepdims=True)
                    p_scr[par, idx, :, :win] = jnp.exp2(s - m).astype(bf16)
                    maxes.append(m)
                m_scr[par, jj] = jnp.where(first_head, maxes[0], maxes[1])

        def values(step, par):
            for jj, row, cls, q0, w0, start in blocks(step):
                r_a = jnp.dot(p_scr[par, 2 * jj, :, :win], src(3, start, win),
                              preferred_element_type=f32)
                r_b = jnp.dot(p_scr[par, 2 * jj + 1, :, :win], src(4, start, win),
                              preferred_element_type=f32)
                o_un = jnp.where(first_head, r_a, r_b)
                den = pltpu.roll(jnp.where(first_head, r_b, r_a), HEAD_DIM, axis=1)
                toks = pl.ds(cls + dil * q0, Q_BLOCK, stride=dil) if dil > 1 else pl.ds(row, Q_BLOCK)
                fbuf[out_slot, toks, :] = o_un / den
                fbuf[out_slot + 1, toks, :] = m_scr[par, jj] + jnp.log2(den)

        assert n_steps % 2 == 0 and n_steps >= 2
        scores(0, 0)
        softmax(0)
        scores(1, 1)

        def body(j, carry):
            step = 2 * j
            values(step, 0)
            softmax(1)
            scores(step + 2, 0)
            values(step + 1, 1)
            softmax(0)
            scores(step + 3, 1)
            return carry

        lax.fori_loop(0, (n_steps - 2) // 2, body, 0)
        values(n_steps - 2, 0)
        softmax(1)
        values(n_steps - 1, 1)

    run_group(lambda n, start, size: rows_bf16(b16, (n,), start, size), 16, 0)
    run_group(lambda n, start, size: rows_bf16(a16, (n,), start, size), 4, 2)
    run_group(lambda n, start, size: rows_bf16(inputs[n], (), start, size), 1, 4)

    def merge(c, carry):
        rows = pl.ds(pl.multiple_of(c * COPY_ROWS, COPY_ROWS), COPY_ROWS)
        lse = [fbuf[2 * g + 1, rows, :] for g in range(3)]
        top = jnp.maximum(jnp.maximum(lse[0], lse[1]), lse[2])
        w = [jnp.exp2(l - top) for l in lse]
        num = w[0] * fbuf[0, rows, :] + w[1] * fbuf[2, rows, :] + w[2] * fbuf[4, rows, :]
        out_ref[rows, :] = (num / (w[0] + w[1] + w[2])).astype(out_ref.dtype)
        return carry

    lax.fori_loop(0, n_copy, merge, 0, unroll=COPY_UNROLL)


def _dil_attn(q, k, v, batch, seq):
    words = seq // 2
    pair = pl.BlockSpec((words, LANES), lambda b, c: (b, c))
    head_a = pl.BlockSpec((words, LANES), lambda b, c: (b, 2 * c))
    head_b = pl.BlockSpec((words, LANES), lambda b, c: (b, 2 * c + 1))
    n_tiles = 2 * Q_BLOCKS_PER_STEP
    win = Q_BLOCK + 2 * HALF_BAND
    return pl.pallas_call(
        functools.partial(_dil_attn_kernel, seq=seq),
        grid=(batch, ATTN_WIDTH // LANES),
        in_specs=[head_a, head_b, pair, head_a, head_b],
        out_specs=pl.BlockSpec((seq, LANES), lambda b, c: (b, c)),
        out_shape=jax.ShapeDtypeStruct((batch * seq, ATTN_WIDTH), jnp.bfloat16),
        scratch_shapes=[
            pltpu.VMEM((6, seq, LANES), jnp.float32),
            pltpu.VMEM((5, words, LANES), jnp.uint32),
            pltpu.VMEM((5, words, LANES), jnp.uint32),
            pltpu.VMEM((3, Q_BLOCK, win), jnp.float32),
            pltpu.VMEM((2, n_tiles, Q_BLOCK, win), jnp.float32),
            pltpu.VMEM((2, n_tiles, Q_BLOCK, win), jnp.bfloat16),
            pltpu.VMEM((2, Q_BLOCKS_PER_STEP, Q_BLOCK, LANES), jnp.float32),
        ],
        compiler_params=pltpu.CompilerParams(
            dimension_semantics=("arbitrary", "arbitrary"),
            vmem_limit_bytes=VMEM_LIMIT_BYTES),
        name="dil_attn",
    )(q, q, k, v, v)


def _tail_kernel(x_ref, p_ref, attn_ref, sga_ref, cb_ref, u_ref, uprev_ref, unext_ref,
                 sgc_ref, convw_ref, na_ref, nc_ref, wout_ref, nple_ref, wgate_ref,
                 bgate_ref, wple_ref, nfin_ref, y_ref, *, tiles_per_seq):
    tm = x_ref.shape[0]
    f32 = jnp.float32
    i = pl.program_id(0)

    u = u_ref[...].astype(f32)
    t_in_seq = i % tiles_per_seq
    prev_row = uprev_ref[BF16_SUBLANES - 1:BF16_SUBLANES, :].astype(f32)
    next_row = unext_ref[0:1, :].astype(f32)
    prev_row = jnp.where(t_in_seq == 0, 0.0, prev_row)
    next_row = jnp.where(t_in_seq == tiles_per_seq - 1, 0.0, next_row)
    ridx = lax.broadcasted_iota(jnp.int32, (tm, 1), 0)
    u_before = jnp.where(ridx == 0, prev_row, pltpu.roll(u, 1, axis=0))
    u_after = jnp.where(ridx == tm - 1, next_row, pltpu.roll(u, tm - 1, axis=0))
    cw = convw_ref[...]
    conv = cb_ref[...].astype(f32) * (cw[0:1] * u_before + cw[1:2] * u + cw[2:3] * u_after)

    attn = attn_ref[...].astype(f32)
    merged = jnp.concatenate(
        [attn * _rms_scale(attn) * na_ref[...] * sga_ref[...].astype(f32),
         conv * _rms_scale(conv) * nc_ref[...] * sgc_ref[...].astype(f32)],
        axis=1).astype(jnp.bfloat16)
    x1 = x_ref[...] + jnp.dot(merged, wout_ref[...], preferred_element_type=f32)

    r = (x1 * _rms_scale(x1) * nple_ref[...]).astype(jnp.bfloat16)
    gate = jax.nn.sigmoid(jnp.dot(r, wgate_ref[...], preferred_element_type=f32)
                          + bgate_ref[...])
    emb = jnp.dot(p_ref[...].astype(jnp.bfloat16), wple_ref[...],
                  preferred_element_type=f32)
    x2 = x1 + emb * gate
    y_ref[...] = x2 * _rms_scale(x2) * nfin_ref[...]


def _tail(x2, p2, attn, sga, cb, u, sgc, conv_w, norm_attn, norm_conv,
          w_out, norm_ple, w_gate, b_gate, w_ple, norm_final, seq, tm):
    T = x2.shape[0]
    halo_blocks = T // BF16_SUBLANES
    per_tile = tm // BF16_SUBLANES
    row = lambda i: (i, 0)
    fixed = lambda i: (0, 0)
    prev = lambda i: (jnp.maximum(i * per_tile - 1, 0), 0)
    nxt = lambda i: (jnp.minimum((i + 1) * per_tile, halo_blocks - 1), 0)
    a_spec = pl.BlockSpec((tm, ATTN_WIDTH), row)
    halo = lambda f: pl.BlockSpec((BF16_SUBLANES, CONV_WIDTH), f)
    full = lambda a: pl.BlockSpec(a.shape, fixed)
    return pl.pallas_call(
        functools.partial(_tail_kernel, tiles_per_seq=seq // tm),
        grid=(T // tm,),
        in_specs=[
            pl.BlockSpec((tm, D_MODEL), row),
            pl.BlockSpec((tm, PLE_DIM), row),
            a_spec, a_spec, a_spec, a_spec, halo(prev), halo(nxt), a_spec,
            full(conv_w), full(norm_attn), full(norm_conv), full(w_out),
            full(norm_ple), full(w_gate), full(b_gate), full(w_ple), full(norm_final),
        ],
        out_specs=pl.BlockSpec((tm, D_MODEL), row),
        out_shape=jax.ShapeDtypeStruct((T, D_MODEL), jnp.float32),
        compiler_params=pltpu.CompilerParams(
            dimension_semantics=("arbitrary",),
            vmem_limit_bytes=VMEM_LIMIT_BYTES),
        name="tail",
    )(x2, p2, attn, sga, cb, u, u, u, sgc, conv_w, norm_attn,
      norm_conv, w_out, norm_ple, w_gate, b_gate, w_ple, norm_final)


def _rope_tables(seq):
    half = ROT_DIM // 2
    inv = jnp.power(jnp.float32(ROPE_THETA),
                    -jnp.arange(0, ROT_DIM, 2, dtype=jnp.float32) / ROT_DIM)
    ang = jnp.arange(seq, dtype=jnp.float32)[:, None] * inv[None, :]
    cos, sin = jnp.cos(ang), jnp.sin(ang)
    pad = HEAD_DIM - ROT_DIM
    ones = jnp.ones((seq, pad), jnp.float32)
    zeros = jnp.zeros((seq, pad), jnp.float32)
    zh = jnp.zeros((seq, half), jnp.float32)
    per_head = lambda parts: jnp.tile(jnp.concatenate(parts, axis=1), (1, LANES // HEAD_DIM))
    return (per_head([cos, cos, ones]),
            per_head([-sin, zh, zeros]),
            per_head([zh, sin, zeros]))


def _trunk(x, p, tables, norm_mix, w_in, conv_w, norm_attn, norm_conv, w_out,
           norm_ple, w_gate, b_gate, w_ple, norm_final):
    B, S, _ = x.shape
    assert S % (16 * Q_BLOCK) == 0
    x2 = x.reshape(B * S, D_MODEL)
    p2 = p.reshape(B * S, PLE_DIM)
    cos_t, sinlo_t, sinhi_t = tables
    q, k, v, sga, cb, u, sgc = _in_proj(x2, norm_mix, w_in, cos_t[:S], sinlo_t[:S],
                                        sinhi_t[:S], S, tm=512)
    attn = _dil_attn(q, k, v, B, S)
    y = _tail(x2, p2, attn, sga, cb, u, sgc, conv_w, norm_attn,
              norm_conv, w_out, norm_ple, w_gate, b_gate, w_ple, norm_final, S, tm=512)
    return y.reshape(B, S, D_MODEL)


def kernel(x_prompt, x_sample, p_prompt, p_sample, norm_mix, w_in, conv_w,
           norm_attn_out, norm_conv_out, w_out, norm_ple, w_ple_gate, b_ple_gate,
           w_ple_proj, norm_final):
    depth = w_in.shape[0]
    assert depth == 1, "single-layer trunk"
    bf16 = jnp.bfloat16
    tables = _rope_tables(max(x_prompt.shape[1], x_sample.shape[1]))
    params = (norm_mix[0][None], w_in[0].astype(bf16), conv_w[0],
              norm_attn_out[0][None], norm_conv_out[0][None], w_out[0].astype(bf16),
              norm_ple[0][None], w_ple_gate[0].astype(bf16), b_ple_gate[0][None],
              w_ple_proj[0].astype(bf16), norm_final[None])
    y_prompt = _trunk(x_prompt, p_prompt[0], tables, *params)
    y_sample = _trunk(x_sample, p_sample[0], tables, *params)
    return (y_prompt, y_sample)
```

```python
import functools

import jax
import jax.numpy as jnp
from jax import lax
from jax.experimental import pallas as pl
from jax.experimental.pallas import tpu as pltpu

D_MODEL = 1024
ATTN_WIDTH = 512
CONV_WIDTH = 512
HEAD_DIM = 64
ROT_DIM = 16
ROPE_THETA = 500000.0
HALF_BAND = 64
DILATIONS = (1, 4, 16)
PLE_DIM = 256
EPS = 1e-6
NEG = -1e30
LOG2_E = 1.4426950408889634

LANES = 128
BF16_SUBLANES = 16
Q_BLOCK = 128
Q_BLOCKS_PER_STEP = 4
COPY_ROWS = 128
COPY_UNROLL = 8
VMEM_LIMIT_BYTES = 56 * 1024 * 1024

_TRANS_B = (((1,), (1,)), ((), ()))


def _rms_scale(v):
    return lax.rsqrt(jnp.mean(v * v, axis=-1, keepdims=True) + EPS)


def _in_proj_kernel(x_ref, g_ref, w_ref, cos_ref, sinlo_ref, sinhi_ref,
                    q_ref, k_ref, v_ref, sga_ref, cb_ref, u_ref, sgc_ref):
    x = x_ref[...]
    h = (x * _rms_scale(x) * g_ref[...]).astype(jnp.bfloat16)

    def cols(j):
        return jnp.dot(h, w_ref[:, j * ATTN_WIDTH:(j + 1) * ATTN_WIDTH],
                       preferred_element_type=jnp.float32)

    cos = cos_ref[...]
    sinlo = sinlo_ref[...]
    sinhi = sinhi_ref[...]

    def rope(t):
        outs = []
        for c in range(ATTN_WIDTH // LANES):
            tc = t[:, c * LANES:(c + 1) * LANES]
            up = pltpu.roll(tc, LANES - ROT_DIM // 2, axis=1)
            dn = pltpu.roll(tc, ROT_DIM // 2, axis=1)
            outs.append(tc * cos + up * sinlo + dn * sinhi)
        return jnp.concatenate(outs, axis=1)

    first_head = lax.broadcasted_iota(jnp.int32, (1, LANES), 1) < HEAD_DIM

    def per_head(t, fill):
        outs = []
        for c in range(ATTN_WIDTH // LANES):
            tc = t[:, c * LANES:(c + 1) * LANES]
            outs += [jnp.where(first_head, tc, fill), jnp.where(first_head, fill, tc)]
        return jnp.concatenate(outs, axis=1).astype(jnp.bfloat16)

    def words(t):
        return pltpu.bitcast(t, jnp.uint32)

    q_ref[...] = words(per_head(rope(cols(0)) * (HEAD_DIM ** -0.5 * LOG2_E), 0.0))
    k_ref[...] = words(rope(cols(1)).astype(jnp.bfloat16))
    v_ref[...] = words(per_head(cols(2), 1.0))
    sga_ref[...] = jax.nn.silu(cols(3)).astype(jnp.bfloat16)
    cb_ref[...] = cols(4).astype(jnp.bfloat16)
    u_ref[...] = (cols(5) * cols(6)).astype(jnp.bfloat16)
    sgc_ref[...] = jax.nn.silu(cols(7)).astype(jnp.bfloat16)


def _in_proj(x2, g, w_bf16, cos_t, sinlo_t, sinhi_t, seq, tm):
    T = x2.shape[0]
    nseq = seq // tm
    row = lambda i: (i, 0)
    fixed = lambda i: (0, 0)
    pos = lambda i: (i % nseq, 0)
    out = jax.ShapeDtypeStruct((T, ATTN_WIDTH), jnp.bfloat16)
    wout = jax.ShapeDtypeStruct((T // 2, ATTN_WIDTH), jnp.uint32)
    wout2 = jax.ShapeDtypeStruct((T // 2, 2 * ATTN_WIDTH), jnp.uint32)
    spec = pl.BlockSpec((tm, ATTN_WIDTH), row)
    wspec = pl.BlockSpec((tm // 2, ATTN_WIDTH), row)
    wspec2 = pl.BlockSpec((tm // 2, 2 * ATTN_WIDTH), row)
    return pl.pallas_call(
        _in_proj_kernel,
        grid=(T // tm,),
        in_specs=[
            pl.BlockSpec((tm, D_MODEL), row),
            pl.BlockSpec((1, D_MODEL), fixed),
            pl.BlockSpec(w_bf16.shape, fixed),
            pl.BlockSpec((tm, LANES), pos),
            pl.BlockSpec((tm, LANES), pos),
            pl.BlockSpec((tm, LANES), pos),
        ],
        out_specs=[wspec2, wspec, wspec2, spec, spec, spec, spec],
        out_shape=[wout2, wout, wout2, out, out, out, out],
        compiler_params=pltpu.CompilerParams(
            dimension_semantics=("arbitrary",),
            vmem_limit_bytes=VMEM_LIMIT_BYTES),
        name="in_proj",
    )(x2, g, w_bf16, cos_t, sinlo_t, sinhi_t)


def _dil_attn_kernel(qa_ref, qb_ref, k_ref, va_ref, vb_ref, out_ref, fbuf, a16, b16, bias_scr,
                     s_scr, p_scr, m_scr, *, seq):
    f32, bf16, u32 = jnp.float32, jnp.bfloat16, jnp.uint32
    len4, len16 = seq // 4, seq // 16
    first_head = lax.broadcasted_iota(jnp.int32, (1, LANES), 1) < HEAD_DIM
    inputs = (qa_ref, qb_ref, k_ref, va_ref, vb_ref)

    def rows_bf16(ref, lead, start, size):
        words = pl.ds(pl.multiple_of(start // 2, BF16_SUBLANES // 2), size // 2)
        return pltpu.bitcast(ref[(*lead, words, slice(None))], bf16)

    def split_row_pairs(words, lead, word0, dst, x, row_even, row_odd):
        w = words[(*lead, pl.ds(word0, COPY_ROWS, stride=2), slice(None))]
        even, odd = (pltpu.unpack_elementwise(w, index=i, packed_dtype=bf16,
                                              unpacked_dtype=f32).astype(bf16) for i in (0, 1))
        half = COPY_ROWS // 2
        dst[x, pl.ds(pl.multiple_of(row_even // 2, half), half), :] = pltpu.bitcast(even, u32)
        dst[x, pl.ds(pl.multiple_of(row_odd // 2, half), half), :] = pltpu.bitcast(odd, u32)

    def split4(c, carry):
        chunks = len4 // COPY_ROWS
        pair = c // chunks
        u0 = (c % chunks) * COPY_ROWS
        for x in range(5):
            split_row_pairs(inputs[x], (), pair + 2 * u0, a16, x,
                            2 * pair * len4 + u0, (2 * pair + 1) * len4 + u0)
        return carry

    def split16(c, carry):
        chunks = len16 // COPY_ROWS
        w0 = (c % chunks) * COPY_ROWS
        pair = (c // chunks) % 2
        r4 = c // (2 * chunks)
        for x in range(5):
            split_row_pairs(a16, (x,), r4 * (len4 // 2) + pair + 2 * w0,
                            b16, x, (r4 + 8 * pair) * len16 + w0,
                            (r4 + 8 * pair + 4) * len16 + w0)
        return carry

    n_copy = seq // COPY_ROWS
    lax.fori_loop(0, n_copy // 2, split4, 0, unroll=COPY_UNROLL)
    lax.fori_loop(0, n_copy // 2, split16, 0, unroll=COPY_UNROLL)

    win_max = Q_BLOCK + 2 * HALF_BAND
    rel = (lax.broadcasted_iota(jnp.int32, (Q_BLOCK, win_max), 0)
           - lax.broadcasted_iota(jnp.int32, (Q_BLOCK, win_max), 1))
    for n in range(3):
        bias_scr[n] = jnp.where(jnp.abs(rel + n * HALF_BAND) <= HALF_BAND, 0.0, NEG)

    def run_group(src, dil, out_slot):
        cls_len = seq // dil
        win = min(cls_len, win_max)
        blocks_per_class = cls_len // Q_BLOCK

        n_steps = seq // (Q_BLOCK * Q_BLOCKS_PER_STEP)

        def blocks(step):
            for jj in range(Q_BLOCKS_PER_STEP):
                t = step * Q_BLOCKS_PER_STEP + jj
                row = pl.multiple_of(t * Q_BLOCK, Q_BLOCK)
                cls = t // blocks_per_class
                q0 = (t % blocks_per_class) * Q_BLOCK
                w0 = jnp.clip(q0 - HALF_BAND, 0, cls_len - win)
                start = pl.multiple_of(cls * cls_len + w0, BF16_SUBLANES)
                yield jj, row, cls, q0, w0, start

        def scores(step, par):
            for jj, row, cls, q0, w0, start in blocks(step):
                bias = bias_scr[(q0 - w0) // HALF_BAND, :, :win]
                q_ab = jnp.concatenate([src(0, row, Q_BLOCK), src(1, row, Q_BLOCK)], axis=0)
                s = lax.dot_general(q_ab, src(2, start, win), _TRANS_B,
                                    preferred_element_type=f32)
                s_scr[par, 2 * jj, :, :win] = s[:Q_BLOCK] + bias
                s_scr[par, 2 * jj + 1, :, :win] = s[Q_BLOCK:] + bias

        def softmax(par):
            for jj in range(Q_BLOCKS_PER_STEP):
                maxes = []
                for idx in (2 * jj, 2 * jj + 1):
                    s = s_scr[par, idx, :, :win]
                    m = jnp.max(s, axis=-1, keepdims=True)
                    p_scr[par, idx, :, :win] = jnp.exp2(s - m).astype(bf16)
                    maxes.append(m)
                m_scr[par, jj] = jnp.where(first_head, maxes[0], maxes[1])

        def values(step, par):
            for jj, row, cls, q0, w0, start in blocks(step):
                r_a = jnp.dot(p_scr[par, 2 * jj, :, :win], src(3, start, win),
                              preferred_element_type=f32)
                r_b = jnp.dot(p_scr[par, 2 * jj + 1, :, :win], src(4, start, win),
                              preferred_element_type=f32)
                o_un = jnp.where(first_head, r_a, r_b)
                den = pltpu.roll(jnp.where(first_head, r_b, r_a), HEAD_DIM, axis=1)
                toks = pl.ds(cls + dil * q0, Q_BLOCK, stride=dil) if dil > 1 else pl.ds(row, Q_BLOCK)
                fbuf[out_slot, toks, :] = o_un / den
                fbuf[out_slot + 1, toks, :] = m_scr[par, jj] + jnp.log2(den)

        assert n_steps % 2 == 0 and n_steps >= 2
        scores(0, 0)
        softmax(0)
        scores(1, 1)

        def body(j, carry):
            step = 2 * j
            values(step, 0)
            softmax(1)
            scores(step + 2, 0)
            values(step + 1, 1)
            softmax(0)
            scores(step + 3, 1)
            return carry

        lax.fori_loop(0, (n_steps - 2) // 2, body, 0)
        values(n_steps - 2, 0)
        softmax(1)
        values(n_steps - 1, 1)

    run_group(lambda n, start, size: rows_bf16(b16, (n,), start, size), 16, 0)
    run_group(lambda n, start, size: rows_bf16(a16, (n,), start, size), 4, 2)
    run_group(lambda n, start, size: rows_bf16(inputs[n], (), start, size), 1, 4)

    def merge(c, carry):
        rows = pl.ds(pl.multiple_of(c * COPY_ROWS, COPY_ROWS), COPY_ROWS)
        lse = [fbuf[2 * g + 1, rows, :] for g in range(3)]
        top = jnp.maximum(jnp.maximum(lse[0], lse[1]), lse[2])
        w = [jnp.exp2(l - top) for l in lse]
        num = w[0] * fbuf[0, rows, :] + w[1] * fbuf[2, rows, :] + w[2] * fbuf[4, rows, :]
        out_ref[rows, :] = (num / (w[0] + w[1] + w[2])).astype(out_ref.dtype)
        return carry

    lax.fori_loop(0, n_copy, merge, 0, unroll=COPY_UNROLL)


def _dil_attn(q, k, v, batch, seq):
    words = seq // 2
    pair = pl.BlockSpec((words, LANES), lambda b, c: (b, c))
    head_a = pl.BlockSpec((words, LANES), lambda b, c: (b, 2 * c))
    head_b = pl.BlockSpec((words, LANES), lambda b, c: (b, 2 * c + 1))
    n_tiles = 2 * Q_BLOCKS_PER_STEP
    win = Q_BLOCK + 2 * HALF_BAND
    return pl.pallas_call(
        functools.partial(_dil_attn_kernel, seq=seq),
        grid=(batch, ATTN_WIDTH // LANES),
        in_specs=[head_a, head_b, pair, head_a, head_b],
        out_specs=pl.BlockSpec((seq, LANES), lambda b, c: (b, c)),
        out_shape=jax.ShapeDtypeStruct((batch * seq, ATTN_WIDTH), jnp.bfloat16),
        scratch_shapes=[
            pltpu.VMEM((6, seq, LANES), jnp.float32),
            pltpu.VMEM((5, words, LANES), jnp.uint32),
            pltpu.VMEM((5, words, LANES), jnp.uint32),
            pltpu.VMEM((3, Q_BLOCK, win), jnp.float32),
            pltpu.VMEM((2, n_tiles, Q_BLOCK, win), jnp.float32),
            pltpu.VMEM((2, n_tiles, Q_BLOCK, win), jnp.bfloat16),
            pltpu.VMEM((2, Q_BLOCKS_PER_STEP, Q_BLOCK, LANES), jnp.float32),
        ],
        compiler_params=pltpu.CompilerParams(
            dimension_semantics=("arbitrary", "arbitrary"),
            vmem_limit_bytes=VMEM_LIMIT_BYTES),
        name="dil_attn",
    )(q, q, k, v, v)


def _tail_kernel(x_ref, p_ref, attn_ref, sga_ref, cb_ref, u_ref, uprev_ref, unext_ref,
                 sgc_ref, convw_ref, na_ref, nc_ref, wout_ref, nple_ref, wgate_ref,
                 bgate_ref, wple_ref, nfin_ref, y_ref, *, tiles_per_seq):
    tm = x_ref.shape[0]
    f32 = jnp.float32
    i = pl.program_id(0)

    u = u_ref[...].astype(f32)
    t_in_seq = i % tiles_per_seq
    prev_row = uprev_ref[BF16_SUBLANES - 1:BF16_SUBLANES, :].astype(f32)
    next_row = unext_ref[0:1, :].astype(f32)
    prev_row = jnp.where(t_in_seq == 0, 0.0, prev_row)
    next_row = jnp.where(t_in_seq == tiles_per_seq - 1, 0.0, next_row)
    ridx = lax.broadcasted_iota(jnp.int32, (tm, 1), 0)
    u_before = jnp.where(ridx == 0, prev_row, pltpu.roll(u, 1, axis=0))
    u_after = jnp.where(ridx == tm - 1, next_row, pltpu.roll(u, tm - 1, axis=0))
    cw = convw_ref[...]
    conv = cb_ref[...].astype(f32) * (cw[0:1] * u_before + cw[1:2] * u + cw[2:3] * u_after)

    attn = attn_ref[...].astype(f32)
    merged = jnp.concatenate(
        [attn * _rms_scale(attn) * na_ref[...] * sga_ref[...].astype(f32),
         conv * _rms_scale(conv) * nc_ref[...] * sgc_ref[...].astype(f32)],
        axis=1).astype(jnp.bfloat16)
    x1 = x_ref[...] + jnp.dot(merged, wout_ref[...], preferred_element_type=f32)

    r = (x1 * _rms_scale(x1) * nple_ref[...]).astype(jnp.bfloat16)
    gate = jax.nn.sigmoid(jnp.dot(r, wgate_ref[...], preferred_element_type=f32)
                          + bgate_ref[...])
    emb = jnp.dot(p_ref[...].astype(jnp.bfloat16), wple_ref[...],
                  preferred_element_type=f32)
    x2 = x1 + emb * gate
    y_ref[...] = x2 * _rms_scale(x2) * nfin_ref[...]


def _tail(x2, p2, attn, sga, cb, u, sgc, conv_w, norm_attn, norm_conv,
          w_out, norm_ple, w_gate, b_gate, w_ple, norm_final, seq, tm):
    T = x2.shape[0]
    halo_blocks = T // BF16_SUBLANES
    per_tile = tm // BF16_SUBLANES
    row = lambda i: (i, 0)
    fixed = lambda i: (0, 0)
    prev = lambda i: (jnp.maximum(i * per_tile - 1, 0), 0)
    nxt = lambda i: (jnp.minimum((i + 1) * per_tile, halo_blocks - 1), 0)
    a_spec = pl.BlockSpec((tm, ATTN_WIDTH), row)
    halo = lambda f: pl.BlockSpec((BF16_SUBLANES, CONV_WIDTH), f)
    full = lambda a: pl.BlockSpec(a.shape, fixed)
    return pl.pallas_call(
        functools.partial(_tail_kernel, tiles_per_seq=seq // tm),
        grid=(T // tm,),
        in_specs=[
            pl.BlockSpec((tm, D_MODEL), row),
            pl.BlockSpec((tm, PLE_DIM), row),
            a_spec, a_spec, a_spec, a_spec, halo(prev), halo(nxt), a_spec,
            full(conv_w), full(norm_attn), full(norm_conv), full(w_out),
            full(norm_ple), full(w_gate), full(b_gate), full(w_ple), full(norm_final),
        ],
        out_specs=pl.BlockSpec((tm, D_MODEL), row),
        out_shape=jax.ShapeDtypeStruct((T, D_MODEL), jnp.float32),
        compiler_params=pltpu.CompilerParams(
            dimension_semantics=("arbitrary",),
            vmem_limit_bytes=VMEM_LIMIT_BYTES),
        name="tail",
    )(x2, p2, attn, sga, cb, u, u, u, sgc, conv_w, norm_attn,
      norm_conv, w_out, norm_ple, w_gate, b_gate, w_ple, norm_final)


def _rope_tables(seq):
    half = ROT_DIM // 2
    inv = jnp.power(jnp.float32(ROPE_THETA),
                    -jnp.arange(0, ROT_DIM, 2, dtype=jnp.float32) / ROT_DIM)
    ang = jnp.arange(seq, dtype=jnp.float32)[:, None] * inv[None, :]
    cos, sin = jnp.cos(ang), jnp.sin(ang)
    pad = HEAD_DIM - ROT_DIM
    ones = jnp.ones((seq, pad), jnp.float32)
    zeros = jnp.zeros((seq, pad), jnp.float32)
    zh = jnp.zeros((seq, half), jnp.float32)
    per_head = lambda parts: jnp.tile(jnp.concatenate(parts, axis=1), (1, LANES // HEAD_DIM))
    return (per_head([cos, cos, ones]),
            per_head([-sin, zh, zeros]),
            per_head([zh, sin, zeros]))


def _trunk(x, p, tables, norm_mix, w_in, conv_w, norm_attn, norm_conv, w_out,
           norm_ple, w_gate, b_gate, w_ple, norm_final):
    B, S, _ = x.shape
    assert S % (16 * Q_BLOCK) == 0
    x2 = x.reshape(B * S, D_MODEL)
    p2 = p.reshape(B * S, PLE_DIM)
    cos_t, sinlo_t, sinhi_t = tables
    q, k, v, sga, cb, u, sgc = _in_proj(x2, norm_mix, w_in, cos_t[:S], sinlo_t[:S],
                                        sinhi_t[:S], S, tm=512)
    attn = _dil_attn(q, k, v, B, S)
    y = _tail(x2, p2, attn, sga, cb, u, sgc, conv_w, norm_attn,
              norm_conv, w_out, norm_ple, w_gate, b_gate, w_ple, norm_final, S, tm=512)
    return y.reshape(B, S, D_MODEL)


def kernel(x_prompt, x_sample, p_prompt, p_sample, norm_mix, w_in, conv_w,
           norm_attn_out, norm_conv_out, w_out, norm_ple, w_ple_gate, b_ple_gate,
           w_ple_proj, norm_final):
    depth = w_in.shape[0]
    assert depth == 1, "single-layer trunk"
    bf16 = jnp.bfloat16
    tables = _rope_tables(max(x_prompt.shape[1], x_sample.shape[1]))
    params = (norm_mix[0][None], w_in[0].astype(bf16), conv_w[0],
              norm_attn_out[0][None], norm_conv_out[0][None], w_out[0].astype(bf16),
              norm_ple[0][None], w_ple_gate[0].astype(bf16), b_ple_gate[0][None],
              w_ple_proj[0].astype(bf16), norm_final[None])
    y_prompt = _trunk(x_prompt, p_prompt[0], tables, *params)
    y_sample = _trunk(x_sample, p_sample[0], tables, *params)
    return (y_prompt, y_sample)
```

```python
import functools

import jax
import jax.numpy as jnp
from jax import lax
from jax.experimental import pallas as pl
from jax.experimental.pallas import tpu as pltpu

D_MODEL = 1024
ATTN_WIDTH = 512
CONV_WIDTH = 512
HEAD_DIM = 64
ROT_DIM = 16
ROPE_THETA = 500000.0
HALF_BAND = 64
DILATIONS = (1, 4, 16)
PLE_DIM = 256
EPS = 1e-6
NEG = -1e30
LOG2_E = 1.4426950408889634

LANES = 128
BF16_SUBLANES = 16
Q_BLOCK = 128
Q_BLOCKS_PER_STEP = 4
COPY_ROWS = 128
COPY_UNROLL = 8
VMEM_LIMIT_BYTES = 56 * 1024 * 1024

_TRANS_B = (((1,), (1,)), ((), ()))


def _rms_scale(v):
    return lax.rsqrt(jnp.mean(v * v, axis=-1, keepdims=True) + EPS)


def _in_proj_kernel(x_ref, g_ref, w_ref, cos_ref, sinlo_ref, sinhi_ref,
                    q_ref, k_ref, v_ref, sga_ref, cb_ref, u_ref, sgc_ref):
    x = x_ref[...]
    h = (x * _rms_scale(x) * g_ref[...]).astype(jnp.bfloat16)

    def cols(j):
        return jnp.dot(h, w_ref[:, j * ATTN_WIDTH:(j + 1) * ATTN_WIDTH],
                       preferred_element_type=jnp.float32)

    cos = cos_ref[...]
    sinlo = sinlo_ref[...]
    sinhi = sinhi_ref[...]

    def rope(t):
        outs = []
        for c in range(ATTN_WIDTH // LANES):
            tc = t[:, c * LANES:(c + 1) * LANES]
            up = pltpu.roll(tc, LANES - ROT_DIM // 2, axis=1)
            dn = pltpu.roll(tc, ROT_DIM // 2, axis=1)
            outs.append(tc * cos + up * sinlo + dn * sinhi)
        return jnp.concatenate(outs, axis=1)

    first_head = lax.broadcasted_iota(jnp.int32, (1, LANES), 1) < HEAD_DIM

    def per_head(t, fill):
        outs = []
        for c in range(ATTN_WIDTH // LANES):
            tc = t[:, c * LANES:(c + 1) * LANES]
            outs += [jnp.where(first_head, tc, fill), jnp.where(first_head, fill, tc)]
        return jnp.concatenate(outs, axis=1).astype(jnp.bfloat16)

    def words(t):
        return pltpu.bitcast(t, jnp.uint32)

    q_ref[...] = words(per_head(rope(cols(0)) * (HEAD_DIM ** -0.5 * LOG2_E), 0.0))
    k_ref[...] = words(rope(cols(1)).astype(jnp.bfloat16))
    v_ref[...] = words(per_head(cols(2), 1.0))
    sga_ref[...] = jax.nn.silu(cols(3)).astype(jnp.bfloat16)
    cb_ref[...] = cols(4).astype(jnp.bfloat16)
    u_ref[...] = (cols(5) * cols(6)).astype(jnp.bfloat16)
    sgc_ref[...] = jax.nn.silu(cols(7)).astype(jnp.bfloat16)


def _in_proj(x2, g, w_bf16, cos_t, sinlo_t, sinhi_t, seq, tm):
    T = x2.shape[0]
    nseq = seq // tm
    row = lambda i: (i, 0)
    fixed = lambda i: (0, 0)
    pos = lambda i: (i % nseq, 0)
    out = jax.ShapeDtypeStruct((T, ATTN_WIDTH), jnp.bfloat16)
    wout = jax.ShapeDtypeStruct((T // 2, ATTN_WIDTH), jnp.uint32)
    wout2 = jax.ShapeDtypeStruct((T // 2, 2 * ATTN_WIDTH), jnp.uint32)
    spec = pl.BlockSpec((tm, ATTN_WIDTH), row)
    wspec = pl.BlockSpec((tm // 2, ATTN_WIDTH), row)
    wspec2 = pl.BlockSpec((tm // 2, 2 * ATTN_WIDTH), row)
    return pl.pallas_call(
        _in_proj_kernel,
        grid=(T // tm,),
        in_specs=[
            pl.BlockSpec((tm, D_MODEL), row),
            pl.BlockSpec((1, D_MODEL), fixed),
            pl.BlockSpec(w_bf16.shape, fixed),
            pl.BlockSpec((tm, LANES), pos),
            pl.BlockSpec((tm, LANES), pos),
            pl.BlockSpec((tm, LANES), pos),
        ],
        out_specs=[wspec2, wspec, wspec2, spec, spec, spec, spec],
        out_shape=[wout2, wout, wout2, out, out, out, out],
        compiler_params=pltpu.CompilerParams(
            dimension_semantics=("arbitrary",),
            vmem_limit_bytes=VMEM_LIMIT_BYTES),
        name="in_proj",
    )(x2, g, w_bf16, cos_t, sinlo_t, sinhi_t)


def _dil_attn_kernel(qa_ref, qb_ref, k_ref, va_ref, vb_ref, out_ref, fbuf, a16, b16, bias_scr,
                     s_scr, p_scr, m_scr, *, seq):
    f32, bf16, u32 = jnp.float32, jnp.bfloat16, jnp.uint32
    len4, len16 = seq // 4, seq // 16
    first_head = lax.broadcasted_iota(jnp.int32, (1, LANES), 1) < HEAD_DIM
    inputs = (qa_ref, qb_ref, k_ref, va_ref, vb_ref)

    def rows_bf16(ref, lead, start, size):
        words = pl.ds(pl.multiple_of(start // 2, BF16_SUBLANES // 2), size // 2)
        return pltpu.bitcast(ref[(*lead, words, slice(None))], bf16)

    def split_row_pairs(words, lead, word0, dst, x, row_even, row_odd):
        w = words[(*lead, pl.ds(word0, COPY_ROWS, stride=2), slice(None))]
        even, odd = (pltpu.unpack_elementwise(w, index=i, packed_dtype=bf16,
                                              unpacked_dtype=f32).astype(bf16) for i in (0, 1))
        half = COPY_ROWS // 2
        dst[x, pl.ds(pl.multiple_of(row_even // 2, half), half), :] = pltpu.bitcast(even, u32)
        dst[x, pl.ds(pl.multiple_of(row_odd // 2, half), half), :] = pltpu.bitcast(odd, u32)

    def split4(c, carry):
        chunks = len4 // COPY_ROWS
        pair = c // chunks
        u0 = (c % chunks) * COPY_ROWS
        for x in range(5):
            split_row_pairs(inputs[x], (), pair + 2 * u0, a16, x,
                            2 * pair * len4 + u0, (2 * pair + 1) * len4 + u0)
        return carry

    def split16(c, carry):
        chunks = len16 // COPY_ROWS
        w0 = (c % chunks) * COPY_ROWS
        pair = (c // chunks) % 2
        r4 = c // (2 * chunks)
        for x in range(5):
            split_row_pairs(a16, (x,), r4 * (len4 // 2) + pair + 2 * w0,
                            b16, x, (r4 + 8 * pair) * len16 + w0,
                            (r4 + 8 * pair + 4) * len16 + w0)
        return carry

    n_copy = seq // COPY_ROWS
    lax.fori_loop(0, n_copy // 2, split4, 0, unroll=COPY_UNROLL)
    lax.fori_loop(0, n_copy // 2, split16, 0, unroll=COPY_UNROLL)

    win_max = Q_BLOCK + 2 * HALF_BAND
    rel = (lax.broadcasted_iota(jnp.int32, (Q_BLOCK, win_max), 0)
           - lax.broadcasted_iota(jnp.int32, (Q_BLOCK, win_max), 1))
    for n in range(3):
        bias_scr[n] = jnp.where(jnp.abs(rel + n * HALF_BAND) <= HALF_BAND, 0.0, NEG)

    def run_group(src, dil, out_slot):
        cls_len = seq // dil
        win = min(cls_len, win_max)
        blocks_per_class = cls_len // Q_BLOCK

        n_steps = seq // (Q_BLOCK * Q_BLOCKS_PER_STEP)

        def blocks(step):
            for jj in range(Q_BLOCKS_PER_STEP):
                t = step * Q_BLOCKS_PER_STEP + jj
                row = pl.multiple_of(t * Q_BLOCK, Q_BLOCK)
                cls = t // blocks_per_class
                q0 = (t % blocks_per_class) * Q_BLOCK
                w0 = jnp.clip(q0 - HALF_BAND, 0, cls_len - win)
                start = pl.multiple_of(cls * cls_len + w0, BF16_SUBLANES)
                yield jj, row, cls, q0, w0, start

        def scores(step, par):
            for jj, row, cls, q0, w0, start in blocks(step):
                bias = bias_scr[(q0 - w0) // HALF_BAND, :, :win]
                q_ab = jnp.concatenate([src(0, row, Q_BLOCK), src(1, row, Q_BLOCK)], axis=0)
                s = lax.dot_general(q_ab, src(2, start, win), _TRANS_B,
                                    preferred_element_type=f32)
                s_scr[par, 2 * jj, :, :win] = s[:Q_BLOCK] + bias
                s_scr[par, 2 * jj + 1, :, :win] = s[Q_BLOCK:] + bias

        def softmax(par):
            for jj in range(Q_BLOCKS_PER_STEP):
                maxes = []
                for idx in (2 * jj, 2 * jj + 1):
                    s = s_scr[par, idx, :, :win]
                    m = jnp.max(s, axis=-1, keepdims=True)
                    p_scr[par, idx, :, :win] = jnp.exp2(s - m).astype(bf16)
                    maxes.append(m)
                m_scr[par, jj] = jnp.where(first_head, maxes[0], maxes[1])

        def values(step, par):
            for jj, row, cls, q0, w0, start in blocks(step):
                r_a = jnp.dot(p_scr[par, 2 * jj, :, :win], src(3, start, win),
                              preferred_element_type=f32)
                r_b = jnp.dot(p_scr[par, 2 * jj + 1, :, :win], src(4, start, win),
                              preferred_element_type=f32)
                o_un = jnp.where(first_head, r_a, r_b)
                den = pltpu.roll(jnp.where(first_head, r_b, r_a), HEAD_DIM, axis=1)
                toks = pl.ds(cls + dil * q0, Q_BLOCK, stride=dil) if dil > 1 else pl.ds(row, Q_BLOCK)
                fbuf[out_slot, toks, :] = o_un / den
                fbuf[out_slot + 1, toks, :] = m_scr[par, jj] + jnp.log2(den)

        assert n_steps % 2 == 0 and n_steps >= 2
        scores(0, 0)
        softmax(0)
        scores(1, 1)

        def body(j, carry):
            step = 2 * j
            values(step, 0)
            softmax(1)
            scores(step + 2, 0)
            values(step + 1, 1)
            softmax(0)
            scores(step + 3, 1)
            return carry

        lax.fori_loop(0, (n_steps - 2) // 2, body, 0)
        values(n_steps - 2, 0)
        softmax(1)
        values(n_steps - 1, 1)

    run_group(lambda n, start, size: rows_bf16(b16, (n,), start, size), 16, 0)
    run_group(lambda n, start, size: rows_bf16(a16, (n,), start, size), 4, 2)
    run_group(lambda n, start, size: rows_bf16(inputs[n], (), start, size), 1, 4)

    def merge(c, carry):
        rows = pl.ds(pl.multiple_of(c * COPY_ROWS, COPY_ROWS), COPY_ROWS)
        lse = [fbuf[2 * g + 1, rows, :] for g in range(3)]
        top = jnp.maximum(jnp.maximum(lse[0], lse[1]), lse[2])
        w = [jnp.exp2(l - top) for l in lse]
        num = w[0] * fbuf[0, rows, :] + w[1] * fbuf[2, rows, :] + w[2] * fbuf[4, rows, :]
        out_ref[rows, :] = (num / (w[0] + w[1] + w[2])).astype(out_ref.dtype)
        return carry

    lax.fori_loop(0, n_copy, merge, 0, unroll=COPY_UNROLL)


def _dil_attn(q, k, v, batch, seq):
    words = seq // 2
    pair = pl.BlockSpec((words, LANES), lambda b, c: (b, c))
    head_a = pl.BlockSpec((words, LANES), lambda b, c: (b, 2 * c))
    head_b = pl.BlockSpec((words, LANES), lambda b, c: (b, 2 * c + 1))
    n_tiles = 2 * Q_BLOCKS_PER_STEP
    win = Q_BLOCK + 2 * HALF_BAND
    return pl.pallas_call(
        functools.partial(_dil_attn_kernel, seq=seq),
        grid=(batch, ATTN_WIDTH // LANES),
        in_specs=[head_a, head_b, pair, head_a, head_b],
        out_specs=pl.BlockSpec((seq, LANES), lambda b, c: (b, c)),
        out_shape=jax.ShapeDtypeStruct((batch * seq, ATTN_WIDTH), jnp.bfloat16),
        scratch_shapes=[
            pltpu.VMEM((6, seq, LANES), jnp.float32),
            pltpu.VMEM((5, words, LANES), jnp.uint32),
            pltpu.VMEM((5, words, LANES), jnp.uint32),
            pltpu.VMEM((3, Q_BLOCK, win), jnp.float32),
            pltpu.VMEM((2, n_tiles, Q_BLOCK, win), jnp.float32),
            pltpu.VMEM((2, n_tiles, Q_BLOCK, win), jnp.bfloat16),
            pltpu.VMEM((2, Q_BLOCKS_PER_STEP, Q_BLOCK, LANES), jnp.float32),
        ],
        compiler_params=pltpu.CompilerParams(
            dimension_semantics=("arbitrary", "arbitrary"),
            vmem_limit_bytes=VMEM_LIMIT_BYTES),
        name="dil_attn",
    )(q, q, k, v, v)


def _tail_kernel(x_ref, p_ref, attn_ref, sga_ref, cb_ref, u_ref, uprev_ref, unext_ref,
                 sgc_ref, convw_ref, na_ref, nc_ref, wout_ref, nple_ref, wgate_ref,
                 bgate_ref, wple_ref, nfin_ref, y_ref, *, tiles_per_seq):
    tm = x_ref.shape[0]
    f32 = jnp.float32
    i = pl.program_id(0)

    u = u_ref[...].astype(f32)
    t_in_seq = i % tiles_per_seq
    prev_row = uprev_ref[BF16_SUBLANES - 1:BF16_SUBLANES, :].astype(f32)
    next_row = unext_ref[0:1, :].astype(f32)
    prev_row = jnp.where(t_in_seq == 0, 0.0, prev_row)
    next_row = jnp.where(t_in_seq == tiles_per_seq - 1, 0.0, next_row)
    ridx = lax.broadcasted_iota(jnp.int32, (tm, 1), 0)
    u_before = jnp.where(ridx == 0, prev_row, pltpu.roll(u, 1, axis=0))
    u_after = jnp.where(ridx == tm - 1, next_row, pltpu.roll(u, tm - 1, axis=0))
    cw = convw_ref[...]
    conv = cb_ref[...].astype(f32) * (cw[0:1] * u_before + cw[1:2] * u + cw[2:3] * u_after)

    attn = attn_ref[...].astype(f32)
    merged = jnp.concatenate(
        [attn * _rms_scale(attn) * na_ref[...] * sga_ref[...].astype(f32),
         conv * _rms_scale(conv) * nc_ref[...] * sgc_ref[...].astype(f32)],
        axis=1).astype(jnp.bfloat16)
    x1 = x_ref[...] + jnp.dot(merged, wout_ref[...], preferred_element_type=f32)

    r = (x1 * _rms_scale(x1) * nple_ref[...]).astype(jnp.bfloat16)
    gate = jax.nn.sigmoid(jnp.dot(r, wgate_ref[...], preferred_element_type=f32)
                          + bgate_ref[...])
    emb = jnp.dot(p_ref[...].astype(jnp.bfloat16), wple_ref[...],
                  preferred_element_type=f32)
    x2 = x1 + emb * gate
    y_ref[...] = x2 * _rms_scale(x2) * nfin_ref[...]


def _tail(x2, p2, attn, sga, cb, u, sgc, conv_w, norm_attn, norm_conv,
          w_out, norm_ple, w_gate, b_gate, w_ple, norm_final, seq, tm):
    T = x2.shape[0]
    halo_blocks = T // BF16_SUBLANES
    per_tile = tm // BF16_SUBLANES
    row = lambda i: (i, 0)
    fixed = lambda i: (0, 0)
    prev = lambda i: (jnp.maximum(i * per_tile - 1, 0), 0)
    nxt = lambda i: (jnp.minimum((i + 1) * per_tile, halo_blocks - 1), 0)
    a_spec = pl.BlockSpec((tm, ATTN_WIDTH), row)
    halo = lambda f: pl.BlockSpec((BF16_SUBLANES, CONV_WIDTH), f)
    full = lambda a: pl.BlockSpec(a.shape, fixed)
    return pl.pallas_call(
        functools.partial(_tail_kernel, tiles_per_seq=seq // tm),
        grid=(T // tm,),
        in_specs=[
            pl.BlockSpec((tm, D_MODEL), row),
            pl.BlockSpec((tm, PLE_DIM), row),
            a_spec, a_spec, a_spec, a_spec, halo(prev), halo(nxt), a_spec,
            full(conv_w), full(norm_attn), full(norm_conv), full(w_out),
            full(norm_ple), full(w_gate), full(b_gate), full(w_ple), full(norm_final),
        ],
        out_specs=pl.BlockSpec((tm, D_MODEL), row),
        out_shape=jax.ShapeDtypeStruct((T, D_MODEL), jnp.float32),
        compiler_params=pltpu.CompilerParams(
            dimension_semantics=("arbitrary",),
            vmem_limit_bytes=VMEM_LIMIT_BYTES),
        name="tail",
    )(x2, p2, attn, sga, cb, u, u, u, sgc, conv_w, norm_attn,
      norm_conv, w_out, norm_ple, w_gate, b_gate, w_ple, norm_final)


def _rope_tables(seq):
    half = ROT_DIM // 2
    inv = jnp.power(jnp.float32(ROPE_THETA),
                    -jnp.arange(0, ROT_DIM, 2, dtype=jnp.float32) / ROT_DIM)
    ang = jnp.arange(seq, dtype=jnp.float32)[:, None] * inv[None, :]
    cos, sin = jnp.cos(ang), jnp.sin(ang)
    pad = HEAD_DIM - ROT_DIM
    ones = jnp.ones((seq, pad), jnp.float32)
    zeros = jnp.zeros((seq, pad), jnp.float32)
    zh = jnp.zeros((seq, half), jnp.float32)
    per_head = lambda parts: jnp.tile(jnp.concatenate(parts, axis=1), (1, LANES // HEAD_DIM))
    return (per_head([cos, cos, ones]),
            per_head([-sin, zh, zeros]),
            per_head([zh, sin, zeros]))


def _trunk(x, p, tables, norm_mix, w_in, conv_w, norm_attn, norm_conv, w_out,
           norm_ple, w_gate, b_gate, w_ple, norm_final):
    B, S, _ = x.shape
    assert S % (16 * Q_BLOCK) == 0
    x2 = x.reshape(B * S, D_MODEL)
    p2 = p.reshape(B * S, PLE_DIM)
    cos_t, sinlo_t, sinhi_t = tables
    q, k, v, sga, cb, u, sgc = _in_proj(x2, norm_mix, w_in, cos_t[:S], sinlo_t[:S],
                                        sinhi_t[:S], S, tm=1024)
    attn = _dil_attn(q, k, v, B, S)
    y = _tail(x2, p2, attn, sga, cb, u, sgc, conv_w, norm_attn,
              norm_conv, w_out, norm_ple, w_gate, b_gate, w_ple, norm_final, S, tm=1024)
    return y.reshape(B, S, D_MODEL)


def kernel(x_prompt, x_sample, p_prompt, p_sample, norm_mix, w_in, conv_w,
           norm_attn_out, norm_conv_out, w_out, norm_ple, w_ple_gate, b_ple_gate,
           w_ple_proj, norm_final):
    depth = w_in.shape[0]
    assert depth == 1, "single-layer trunk"
    bf16 = jnp.bfloat16
    tables = _rope_tables(max(x_prompt.shape[1], x_sample.shape[1]))
    params = (norm_mix[0][None], w_in[0].astype(bf16), conv_w[0],
              norm_attn_out[0][None], norm_conv_out[0][None], w_out[0].astype(bf16),
              norm_ple[0][None], w_ple_gate[0].astype(bf16), b_ple_gate[0][None],
              w_ple_proj[0].astype(bf16), norm_final[None])
    y_prompt = _trunk(x_prompt, p_prompt[0], tables, *params)
    y_sample = _trunk(x_sample, p_sample[0], tables, *params)
    return (y_prompt, y_sample)
```

```python
import functools

import jax
import jax.numpy as jnp
from jax import lax
from jax.experimental import pallas as pl
from jax.experimental.pallas import tpu as pltpu

D_MODEL = 1024
ATTN_WIDTH = 512
CONV_WIDTH = 512
HEAD_DIM = 64
ROT_DIM = 16
ROPE_THETA = 500000.0
HALF_BAND = 64
DILATIONS = (1, 4, 16)
PLE_DIM = 256
EPS = 1e-6
NEG = -1e30
LOG2_E = 1.4426950408889634

LANES = 128
BF16_SUBLANES = 16
Q_BLOCK = 128
Q_BLOCKS_PER_STEP = 4
COPY_ROWS = 128
COPY_UNROLL = 8
VMEM_LIMIT_BYTES = 56 * 1024 * 1024

_TRANS_B = (((1,), (1,)), ((), ()))


def _rms_scale(v):
    return lax.rsqrt(jnp.mean(v * v, axis=-1, keepdims=True) + EPS)


def _in_proj_kernel(x_ref, g_ref, w_ref, cos_ref, sinlo_ref, sinhi_ref,
                    q_ref, k_ref, v_ref, sga_ref, cb_ref, u_ref, sgc_ref):
    x = x_ref[...]
    h = (x * _rms_scale(x) * g_ref[...]).astype(jnp.bfloat16)

    def cols(j):
        return jnp.dot(h, w_ref[:, j * ATTN_WIDTH:(j + 1) * ATTN_WIDTH],
                       preferred_element_type=jnp.float32)

    cos = cos_ref[...]
    sinlo = sinlo_ref[...]
    sinhi = sinhi_ref[...]

    def rope(t):
        outs = []
        for c in range(ATTN_WIDTH // LANES):
            tc = t[:, c * LANES:(c + 1) * LANES]
            up = pltpu.roll(tc, LANES - ROT_DIM // 2, axis=1)
            dn = pltpu.roll(tc, ROT_DIM // 2, axis=1)
            outs.append(tc * cos + up * sinlo + dn * sinhi)
        return jnp.concatenate(outs, axis=1)

    first_head = lax.broadcasted_iota(jnp.int32, (1, LANES), 1) < HEAD_DIM

    def per_head(t, fill):
        outs = []
        for c in range(ATTN_WIDTH // LANES):
            tc = t[:, c * LANES:(c + 1) * LANES]
            outs += [jnp.where(first_head, tc, fill), jnp.where(first_head, fill, tc)]
        return jnp.concatenate(outs, axis=1).astype(jnp.bfloat16)

    def words(t):
        return pltpu.bitcast(t, jnp.uint32)

    q_ref[...] = words(per_head(rope(cols(0)) * (HEAD_DIM ** -0.5 * LOG2_E), 0.0))
    k_ref[...] = words(rope(cols(1)).astype(jnp.bfloat16))
    v_ref[...] = words(per_head(cols(2), 1.0))
    sga_ref[...] = jax.nn.silu(cols(3)).astype(jnp.bfloat16)
    cb_ref[...] = cols(4).astype(jnp.bfloat16)
    u_ref[...] = (cols(5) * cols(6)).astype(jnp.bfloat16)
    sgc_ref[...] = jax.nn.silu(cols(7)).astype(jnp.bfloat16)


def _in_proj(x2, g, w_bf16, cos_t, sinlo_t, sinhi_t, seq, tm):
    T = x2.shape[0]
    nseq = seq // tm
    row = lambda i: (i, 0)
    fixed = lambda i: (0, 0)
    pos = lambda i: (i % nseq, 0)
    out = jax.ShapeDtypeStruct((T, ATTN_WIDTH), jnp.bfloat16)
    wout = jax.ShapeDtypeStruct((T // 2, ATTN_WIDTH), jnp.uint32)
    wout2 = jax.ShapeDtypeStruct((T // 2, 2 * ATTN_WIDTH), jnp.uint32)
    spec = pl.BlockSpec((tm, ATTN_WIDTH), row)
    wspec = pl.BlockSpec((tm // 2, ATTN_WIDTH), row)
    wspec2 = pl.BlockSpec((tm // 2, 2 * ATTN_WIDTH), row)
    return pl.pallas_call(
        _in_proj_kernel,
        grid=(T // tm,),
        in_specs=[
            pl.BlockSpec((tm, D_MODEL), row),
            pl.BlockSpec((1, D_MODEL), fixed),
            pl.BlockSpec(w_bf16.shape, fixed),
            pl.BlockSpec((tm, LANES), pos),
            pl.BlockSpec((tm, LANES), pos),
            pl.BlockSpec((tm, LANES), pos),
        ],
        out_specs=[wspec2, wspec, wspec2, spec, spec, spec, spec],
        out_shape=[wout2, wout, wout2, out, out, out, out],
        compiler_params=pltpu.CompilerParams(
            dimension_semantics=("arbitrary",),
            vmem_limit_bytes=VMEM_LIMIT_BYTES),
        name="in_proj",
    )(x2, g, w_bf16, cos_t, sinlo_t, sinhi_t)


def _dil_attn_kernel(qa_ref, qb_ref, k_ref, va_ref, vb_ref, out_ref, fbuf, a16, b16, bias_scr,
                     s_scr, p_scr, m_scr, *, seq):
    f32, bf16, u32 = jnp.float32, jnp.bfloat16, jnp.uint32
    len4, len16 = seq // 4, seq // 16
    first_head = lax.broadcasted_iota(jnp.int32, (1, LANES), 1) < HEAD_DIM
    inputs = (qa_ref, qb_ref, k_ref, va_ref, vb_ref)
    n_copy = seq // (2 * COPY_ROWS)

    def rows_bf16(ref, lead, start, size):
        words = pl.ds(pl.multiple_of(start // 2, BF16_SUBLANES // 2), size // 2)
        return pltpu.bitcast(ref[(*lead, words, slice(None))], bf16)

    def split_row_pairs(words, lead, word0, dst, x, row_even, row_odd):
        w = words[(*lead, pl.ds(word0, COPY_ROWS, stride=2), slice(None))]
        even, odd = (pltpu.unpack_elementwise(w, index=i, packed_dtype=bf16,
                                              unpacked_dtype=f32).astype(bf16) for i in (0, 1))
        half = COPY_ROWS // 2
        dst[x, pl.ds(pl.multiple_of(row_even // 2, half), half), :] = pltpu.bitcast(even, u32)
        dst[x, pl.ds(pl.multiple_of(row_odd // 2, half), half), :] = pltpu.bitcast(odd, u32)

    def split4(c):
        chunks = len4 // COPY_ROWS
        pair = c // chunks
        u0 = (c % chunks) * COPY_ROWS
        for x in range(5):
            split_row_pairs(inputs[x], (), pair + 2 * u0, a16, x,
                            2 * pair * len4 + u0, (2 * pair + 1) * len4 + u0)

    def split16(c):
        chunks = len16 // COPY_ROWS
        w0 = (c % chunks) * COPY_ROWS
        pair = (c // chunks) % 2
        r4 = c // (2 * chunks)
        for x in range(5):
            split_row_pairs(a16, (x,), r4 * (len4 // 2) + pair + 2 * w0,
                            b16, x, (r4 + 8 * pair) * len16 + w0,
                            (r4 + 8 * pair + 4) * len16 + w0)

    win_max = Q_BLOCK + 2 * HALF_BAND
    rel = (lax.broadcasted_iota(jnp.int32, (Q_BLOCK, win_max), 0)
           - lax.broadcasted_iota(jnp.int32, (Q_BLOCK, win_max), 1))
    for n in range(3):
        bias_scr[n] = jnp.where(jnp.abs(rel + n * HALF_BAND) <= HALF_BAND, 0.0, NEG)

    def run_group(src, dil, out_slot, relayout=None):
        cls_len = seq // dil
        win = min(cls_len, win_max)
        blocks_per_class = cls_len // Q_BLOCK

        n_steps = seq // (Q_BLOCK * Q_BLOCKS_PER_STEP)

        def blocks(step):
            for jj in range(Q_BLOCKS_PER_STEP):
                t = step * Q_BLOCKS_PER_STEP + jj
                row = pl.multiple_of(t * Q_BLOCK, Q_BLOCK)
                cls = t // blocks_per_class
                q0 = (t % blocks_per_class) * Q_BLOCK
                w0 = jnp.clip(q0 - HALF_BAND, 0, cls_len - win)
                start = pl.multiple_of(cls * cls_len + w0, BF16_SUBLANES)
                yield jj, row, cls, q0, w0, start

        def scores(step, par):
            for jj, row, cls, q0, w0, start in blocks(step):
                bias = bias_scr[(q0 - w0) // HALF_BAND, :, :win]
                q_ab = jnp.concatenate([src(0, row, Q_BLOCK), src(1, row, Q_BLOCK)], axis=0)
                s = lax.dot_general(q_ab, src(2, start, win), _TRANS_B,
                                    preferred_element_type=f32)
                s_scr[par, 2 * jj, :, :win] = s[:Q_BLOCK] + bias
                s_scr[par, 2 * jj + 1, :, :win] = s[Q_BLOCK:] + bias

        def softmax(par):
            for jj in range(Q_BLOCKS_PER_STEP):
                maxes = []
                for idx in (2 * jj, 2 * jj + 1):
                    s = s_scr[par, idx, :, :win]
                    m = jnp.max(s, axis=-1, keepdims=True)
                    p_scr[par, idx, :, :win] = jnp.exp2(s - m).astype(bf16)
                    maxes.append(m)
                m_scr[par, jj] = jnp.where(first_head, maxes[0], maxes[1])

        def values(step, par):
            for jj, row, cls, q0, w0, start in blocks(step):
                r_a = jnp.dot(p_scr[par, 2 * jj, :, :win], src(3, start, win),
                              preferred_element_type=f32)
                r_b = jnp.dot(p_scr[par, 2 * jj + 1, :, :win], src(4, start, win),
                              preferred_element_type=f32)
                o_un = jnp.where(first_head, r_a, r_b)
                den = pltpu.roll(jnp.where(first_head, r_b, r_a), HEAD_DIM, axis=1)
                toks = pl.ds(cls + dil * q0, Q_BLOCK, stride=dil) if dil > 1 else pl.ds(row, Q_BLOCK)
                fbuf[out_slot, toks, :] = o_un / den
                fbuf[out_slot + 1, toks, :] = m_scr[par, jj] + jnp.log2(den)

        assert n_steps % 2 == 0 and n_steps >= 2
        n_loop = (n_steps - 2) // 2
        per_part = n_copy // (n_loop + 2) if relayout else 0

        def fill(first, count):
            for i in range(count):
                relayout(first + i)

        scores(0, 0)
        softmax(0)
        scores(1, 1)
        fill(0, per_part)

        def body(j, carry):
            step = 2 * j
            values(step, 0)
            softmax(1)
            scores(step + 2, 0)
            fill((j + 1) * per_part, per_part)
            values(step + 1, 1)
            softmax(0)
            scores(step + 3, 1)
            return carry

        lax.fori_loop(0, n_loop, body, 0)
        values(n_steps - 2, 0)
        softmax(1)
        if relayout:
            fill((n_loop + 1) * per_part, n_copy - (n_loop + 1) * per_part)
        values(n_steps - 1, 1)

    run_group(lambda n, start, size: rows_bf16(inputs[n], (), start, size), 1, 4, split4)
    run_group(lambda n, start, size: rows_bf16(a16, (n,), start, size), 4, 2, split16)
    run_group(lambda n, start, size: rows_bf16(b16, (n,), start, size), 16, 0)

    def merge(c, carry):
        rows = pl.ds(pl.multiple_of(c * COPY_ROWS, COPY_ROWS), COPY_ROWS)
        lse = [fbuf[2 * g + 1, rows, :] for g in range(3)]
        top = jnp.maximum(jnp.maximum(lse[0], lse[1]), lse[2])
        w = [jnp.exp2(l - top) for l in lse]
        num = w[0] * fbuf[0, rows, :] + w[1] * fbuf[2, rows, :] + w[2] * fbuf[4, rows, :]
        out_ref[rows, :] = (num / (w[0] + w[1] + w[2])).astype(out_ref.dtype)
        return carry

    lax.fori_loop(0, seq // COPY_ROWS, merge, 0, unroll=COPY_UNROLL)


def _dil_attn(q, k, v, batch, seq):
    words = seq // 2
    pair = pl.BlockSpec((words, LANES), lambda b, c: (b, c))
    head_a = pl.BlockSpec((words, LANES), lambda b, c: (b, 2 * c))
    head_b = pl.BlockSpec((words, LANES), lambda b, c: (b, 2 * c + 1))
    n_tiles = 2 * Q_BLOCKS_PER_STEP
    win = Q_BLOCK + 2 * HALF_BAND
    return pl.pallas_call(
        functools.partial(_dil_attn_kernel, seq=seq),
        grid=(batch, ATTN_WIDTH // LANES),
        in_specs=[head_a, head_b, pair, head_a, head_b],
        out_specs=pl.BlockSpec((seq, LANES), lambda b, c: (b, c)),
        out_shape=jax.ShapeDtypeStruct((batch * seq, ATTN_WIDTH), jnp.bfloat16),
        scratch_shapes=[
            pltpu.VMEM((6, seq, LANES), jnp.float32),
            pltpu.VMEM((5, words, LANES), jnp.uint32),
            pltpu.VMEM((5, words, LANES), jnp.uint32),
            pltpu.VMEM((3, Q_BLOCK, win), jnp.float32),
            pltpu.VMEM((2, n_tiles, Q_BLOCK, win), jnp.float32),
            pltpu.VMEM((2, n_tiles, Q_BLOCK, win), jnp.bfloat16),
            pltpu.VMEM((2, Q_BLOCKS_PER_STEP, Q_BLOCK, LANES), jnp.float32),
        ],
        compiler_params=pltpu.CompilerParams(
            dimension_semantics=("arbitrary", "arbitrary"),
            vmem_limit_bytes=VMEM_LIMIT_BYTES),
        name="dil_attn",
    )(q, q, k, v, v)


def _tail_kernel(x_ref, p_ref, attn_ref, sga_ref, cb_ref, u_ref, uprev_ref, unext_ref,
                 sgc_ref, convw_ref, na_ref, nc_ref, wout_ref, nple_ref, wgate_ref,
                 bgate_ref, wple_ref, nfin_ref, y_ref, *, tiles_per_seq):
    tm = x_ref.shape[0]
    f32 = jnp.float32
    i = pl.program_id(0)

    u = u_ref[...].astype(f32)
    t_in_seq = i % tiles_per_seq
    prev_row = uprev_ref[BF16_SUBLANES - 1:BF16_SUBLANES, :].astype(f32)
    next_row = unext_ref[0:1, :].astype(f32)
    prev_row = jnp.where(t_in_seq == 0, 0.0, prev_row)
    next_row = jnp.where(t_in_seq == tiles_per_seq - 1, 0.0, next_row)
    ridx = lax.broadcasted_iota(jnp.int32, (tm, 1), 0)
    u_before = jnp.where(ridx == 0, prev_row, pltpu.roll(u, 1, axis=0))
    u_after = jnp.where(ridx == tm - 1, next_row, pltpu.roll(u, tm - 1, axis=0))
    cw = convw_ref[...]
    conv = cb_ref[...].astype(f32) * (cw[0:1] * u_before + cw[1:2] * u + cw[2:3] * u_after)

    attn = attn_ref[...].astype(f32)
    merged = jnp.concatenate(
        [attn * _rms_scale(attn) * na_ref[...] * sga_ref[...].astype(f32),
         conv * _rms_scale(conv) * nc_ref[...] * sgc_ref[...].astype(f32)],
        axis=1).astype(jnp.bfloat16)
    x1 = x_ref[...] + jnp.dot(merged, wout_ref[...], preferred_element_type=f32)

    r = (x1 * _rms_scale(x1) * nple_ref[...]).astype(jnp.bfloat16)
    gate = jax.nn.sigmoid(jnp.dot(r, wgate_ref[...], preferred_element_type=f32)
                          + bgate_ref[...])
    emb = jnp.dot(p_ref[...].astype(jnp.bfloat16), wple_ref[...],
                  preferred_element_type=f32)
    x2 = x1 + emb * gate
    y_ref[...] = x2 * _rms_scale(x2) * nfin_ref[...]


def _tail(x2, p2, attn, sga, cb, u, sgc, conv_w, norm_attn, norm_conv,
          w_out, norm_ple, w_gate, b_gate, w_ple, norm_final, seq, tm):
    T = x2.shape[0]
    halo_blocks = T // BF16_SUBLANES
    per_tile = tm // BF16_SUBLANES
    row = lambda i: (i, 0)
    fixed = lambda i: (0, 0)
    prev = lambda i: (jnp.maximum(i * per_tile - 1, 0), 0)
    nxt = lambda i: (jnp.minimum((i + 1) * per_tile, halo_blocks - 1), 0)
    a_spec = pl.BlockSpec((tm, ATTN_WIDTH), row)
    halo = lambda f: pl.BlockSpec((BF16_SUBLANES, CONV_WIDTH), f)
    full = lambda a: pl.BlockSpec(a.shape, fixed)
    return pl.pallas_call(
        functools.partial(_tail_kernel, tiles_per_seq=seq // tm),
        grid=(T // tm,),
        in_specs=[
            pl.BlockSpec((tm, D_MODEL), row),
            pl.BlockSpec((tm, PLE_DIM), row),
            a_spec, a_spec, a_spec, a_spec, halo(prev), halo(nxt), a_spec,
            full(conv_w), full(norm_attn), full(norm_conv), full(w_out),
            full(norm_ple), full(w_gate), full(b_gate), full(w_ple), full(norm_final),
        ],
        out_specs=pl.BlockSpec((tm, D_MODEL), row),
        out_shape=jax.ShapeDtypeStruct((T, D_MODEL), jnp.float32),
        compiler_params=pltpu.CompilerParams(
            dimension_semantics=("arbitrary",),
            vmem_limit_bytes=VMEM_LIMIT_BYTES),
        name="tail",
    )(x2, p2, attn, sga, cb, u, u, u, sgc, conv_w, norm_attn,
      norm_conv, w_out, norm_ple, w_gate, b_gate, w_ple, norm_final)


def _rope_tables(seq):
    half = ROT_DIM // 2
    inv = jnp.power(jnp.float32(ROPE_THETA),
                    -jnp.arange(0, ROT_DIM, 2, dtype=jnp.float32) / ROT_DIM)
    ang = jnp.arange(seq, dtype=jnp.float32)[:, None] * inv[None, :]
    cos, sin = jnp.cos(ang), jnp.sin(ang)
    pad = HEAD_DIM - ROT_DIM
    ones = jnp.ones((seq, pad), jnp.float32)
    zeros = jnp.zeros((seq, pad), jnp.float32)
    zh = jnp.zeros((seq, half), jnp.float32)
    per_head = lambda parts: jnp.tile(jnp.concatenate(parts, axis=1), (1, LANES // HEAD_DIM))
    return (per_head([cos, cos, ones]),
            per_head([-sin, zh, zeros]),
            per_head([zh, sin, zeros]))


def _trunk(x, p, tables, norm_mix, w_in, conv_w, norm_attn, norm_conv, w_out,
           norm_ple, w_gate, b_gate, w_ple, norm_final):
    B, S, _ = x.shape
    assert S % (16 * Q_BLOCK) == 0
    x2 = x.reshape(B * S, D_MODEL)
    p2 = p.reshape(B * S, PLE_DIM)
    cos_t, sinlo_t, sinhi_t = tables
    q, k, v, sga, cb, u, sgc = _in_proj(x2, norm_mix, w_in, cos_t[:S], sinlo_t[:S],
                                        sinhi_t[:S], S, tm=1024)
    attn = _dil_attn(q, k, v, B, S)
    y = _tail(x2, p2, attn, sga, cb, u, sgc, conv_w, norm_attn,
              norm_conv, w_out, norm_ple, w_gate, b_gate, w_ple, norm_final, S, tm=1024)
    return y.reshape(B, S, D_MODEL)


def kernel(x_prompt, x_sample, p_prompt, p_sample, norm_mix, w_in, conv_w,
           norm_attn_out, norm_conv_out, w_out, norm_ple, w_ple_gate, b_ple_gate,
           w_ple_proj, norm_final):
    depth = w_in.shape[0]
    assert depth == 1, "single-layer trunk"
    bf16 = jnp.bfloat16
    tables = _rope_tables(max(x_prompt.shape[1], x_sample.shape[1]))
    params = (norm_mix[0][None], w_in[0].astype(bf16), conv_w[0],
              norm_attn_out[0][None], norm_conv_out[0][None], w_out[0].astype(bf16),
              norm_ple[0][None], w_ple_gate[0].astype(bf16), b_ple_gate[0][None],
              w_ple_proj[0].astype(bf16), norm_final[None])
    y_prompt = _trunk(x_prompt, p_prompt[0], tables, *params)
    y_sample = _trunk(x_sample, p_sample[0], tables, *params)
    return (y_prompt, y_sample)
```

```python
import functools

import jax
import jax.numpy as jnp
from jax import lax
from jax.experimental import pallas as pl
from jax.experimental.pallas import tpu as pltpu

D_MODEL = 1024
ATTN_WIDTH = 512
CONV_WIDTH = 512
HEAD_DIM = 64
ROT_DIM = 16
ROPE_THETA = 500000.0
HALF_BAND = 64
DILATIONS = (1, 4, 16)
PLE_DIM = 256
EPS = 1e-6
NEG = -1e30
LOG2_E = 1.4426950408889634

LANES = 128
F32_SUBLANES = 8
BF16_SUBLANES = 16
Q_BLOCK = 128
Q_BLOCKS_PER_STEP = 4
COPY_ROWS = 128
COPY_UNROLL = 8
VMEM_LIMIT_BYTES = 56 * 1024 * 1024

_TRANS_B = (((1,), (1,)), ((), ()))


def _rms_scale(v):
    return lax.rsqrt(jnp.mean(v * v, axis=-1, keepdims=True) + EPS)


def _in_proj_kernel(x_ref, xprev_ref, xnext_ref, g_ref, w_ref, cos_ref, sinlo_ref, sinhi_ref,
                    convw_ref, nconv_ref, q_ref, k_ref, v_ref, sga_ref, conv_ref,
                    *, tiles_per_seq):
    tm = x_ref.shape[0]

    def normed(x):
        return (x * _rms_scale(x) * g_ref[...]).astype(jnp.bfloat16)

    h = normed(x_ref[...])

    def cols(j, hh=None, n=1):
        return jnp.dot(h if hh is None else hh,
                       w_ref[:, j * ATTN_WIDTH:(j + n) * ATTN_WIDTH],
                       preferred_element_type=jnp.float32)

    cos = cos_ref[...]
    sinlo = sinlo_ref[...]
    sinhi = sinhi_ref[...]

    def rope(t):
        outs = []
        for c in range(ATTN_WIDTH // LANES):
            tc = t[:, c * LANES:(c + 1) * LANES]
            up = pltpu.roll(tc, LANES - ROT_DIM // 2, axis=1)
            dn = pltpu.roll(tc, ROT_DIM // 2, axis=1)
            outs.append(tc * cos + up * sinlo + dn * sinhi)
        return jnp.concatenate(outs, axis=1)

    first_head = lax.broadcasted_iota(jnp.int32, (1, LANES), 1) < HEAD_DIM

    def per_head(t, fill):
        outs = []
        for c in range(ATTN_WIDTH // LANES):
            tc = t[:, c * LANES:(c + 1) * LANES]
            outs += [jnp.where(first_head, tc, fill), jnp.where(first_head, fill, tc)]
        return jnp.concatenate(outs, axis=1).astype(jnp.bfloat16)

    def words(t):
        return pltpu.bitcast(t, jnp.uint32)

    def gated(hh):
        cc_ch = cols(5, hh, 2)
        return cc_ch[:, :CONV_WIDTH] * cc_ch[:, CONV_WIDTH:]

    u = gated(h)
    t_in_seq = pl.program_id(0) % tiles_per_seq
    prev_row = gated(normed(xprev_ref[...]))[F32_SUBLANES - 1:F32_SUBLANES, :]
    next_row = gated(normed(xnext_ref[...]))[0:1, :]
    prev_row = jnp.where(t_in_seq == 0, 0.0, prev_row)
    next_row = jnp.where(t_in_seq == tiles_per_seq - 1, 0.0, next_row)

    q_ref[...] = words(per_head(rope(cols(0)) * (HEAD_DIM ** -0.5 * LOG2_E), 0.0))

    ridx = lax.broadcasted_iota(jnp.int32, (tm, 1), 0)
    u_before = jnp.where(ridx == 0, prev_row, pltpu.roll(u, 1, axis=0))
    u_after = jnp.where(ridx == tm - 1, next_row, pltpu.roll(u, tm - 1, axis=0))
    cw = convw_ref[...]
    taps = cw[0:1] * u_before + cw[1:2] * u + cw[2:3] * u_after

    k_ref[...] = words(rope(cols(1)).astype(jnp.bfloat16))

    conv = cols(4) * taps
    conv = conv * _rms_scale(conv) * nconv_ref[...]

    v_ref[...] = words(per_head(cols(2), 1.0))

    conv_ref[...] = (conv * jax.nn.silu(cols(7))).astype(jnp.bfloat16)
    sga_ref[...] = jax.nn.silu(cols(3)).astype(jnp.bfloat16)


def _in_proj(x2, g, w_bf16, cos_t, sinlo_t, sinhi_t, conv_w, norm_conv, seq, tm):
    T = x2.shape[0]
    nseq = seq // tm
    halo_blocks = T // F32_SUBLANES
    per_tile = tm // F32_SUBLANES
    row = lambda i: (i, 0)
    fixed = lambda i: (0, 0)
    pos = lambda i: (i % nseq, 0)
    prev = lambda i: (jnp.maximum(i * per_tile - 1, 0), 0)
    nxt = lambda i: (jnp.minimum((i + 1) * per_tile, halo_blocks - 1), 0)
    out = jax.ShapeDtypeStruct((T, ATTN_WIDTH), jnp.bfloat16)
    wout = jax.ShapeDtypeStruct((T // 2, ATTN_WIDTH), jnp.uint32)
    wout2 = jax.ShapeDtypeStruct((T // 2, 2 * ATTN_WIDTH), jnp.uint32)
    spec = pl.BlockSpec((tm, ATTN_WIDTH), row)
    wspec = pl.BlockSpec((tm // 2, ATTN_WIDTH), row)
    wspec2 = pl.BlockSpec((tm // 2, 2 * ATTN_WIDTH), row)
    full = lambda a: pl.BlockSpec(a.shape, fixed)
    return pl.pallas_call(
        functools.partial(_in_proj_kernel, tiles_per_seq=nseq),
        grid=(T // tm,),
        in_specs=[
            pl.BlockSpec((tm, D_MODEL), row),
            pl.BlockSpec((F32_SUBLANES, D_MODEL), prev),
            pl.BlockSpec((F32_SUBLANES, D_MODEL), nxt),
            full(g), full(w_bf16),
            pl.BlockSpec((tm, LANES), pos),
            pl.BlockSpec((tm, LANES), pos),
            pl.BlockSpec((tm, LANES), pos),
            full(conv_w), full(norm_conv),
        ],
        out_specs=[wspec2, wspec, wspec2, spec, spec],
        out_shape=[wout2, wout, wout2, out, out],
        compiler_params=pltpu.CompilerParams(
            dimension_semantics=("arbitrary",),
            vmem_limit_bytes=VMEM_LIMIT_BYTES),
        name="in_proj",
    )(x2, x2, x2, g, w_bf16, cos_t, sinlo_t, sinhi_t, conv_w, norm_conv)


def _dil_attn_kernel(qa_ref, qb_ref, k_ref, va_ref, vb_ref, out_ref, fbuf, a16, b16, bias_scr,
                     s_scr, p_scr, m_scr, *, seq):
    f32, bf16, u32 = jnp.float32, jnp.bfloat16, jnp.uint32
    len4, len16 = seq // 4, seq // 16
    first_head = lax.broadcasted_iota(jnp.int32, (1, LANES), 1) < HEAD_DIM
    inputs = (qa_ref, qb_ref, k_ref, va_ref, vb_ref)
    n_copy = seq // (2 * COPY_ROWS)

    def rows_bf16(ref, lead, start, size):
        words = pl.ds(pl.multiple_of(start // 2, BF16_SUBLANES // 2), size // 2)
        return pltpu.bitcast(ref[(*lead, words, slice(None))], bf16)

    def split_row_pairs(words, lead, word0, dst, x, row_even, row_odd):
        w = words[(*lead, pl.ds(word0, COPY_ROWS, stride=2), slice(None))]
        even, odd = (pltpu.unpack_elementwise(w, index=i, packed_dtype=bf16,
                                              unpacked_dtype=f32).astype(bf16) for i in (0, 1))
        half = COPY_ROWS // 2
        dst[x, pl.ds(pl.multiple_of(row_even // 2, half), half), :] = pltpu.bitcast(even, u32)
        dst[x, pl.ds(pl.multiple_of(row_odd // 2, half), half), :] = pltpu.bitcast(odd, u32)

    def split4(c):
        chunks = len4 // COPY_ROWS
        pair = c // chunks
        u0 = (c % chunks) * COPY_ROWS
        for x in range(5):
            split_row_pairs(inputs[x], (), pair + 2 * u0, a16, x,
                            2 * pair * len4 + u0, (2 * pair + 1) * len4 + u0)

    def split16(c):
        chunks = len16 // COPY_ROWS
        w0 = (c % chunks) * COPY_ROWS
        pair = (c // chunks) % 2
        r4 = c // (2 * chunks)
        for x in range(5):
            split_row_pairs(a16, (x,), r4 * (len4 // 2) + pair + 2 * w0,
                            b16, x, (r4 + 8 * pair) * len16 + w0,
                            (r4 + 8 * pair + 4) * len16 + w0)

    win_max = Q_BLOCK + 2 * HALF_BAND
    rel = (lax.broadcasted_iota(jnp.int32, (Q_BLOCK, win_max), 0)
           - lax.broadcasted_iota(jnp.int32, (Q_BLOCK, win_max), 1))
    for n in range(3):
        bias_scr[n] = jnp.where(jnp.abs(rel + n * HALF_BAND) <= HALF_BAND, 0.0, NEG)

    def run_group(src, dil, out_slot, relayout=None):
        cls_len = seq // dil
        win = min(cls_len, win_max)
        blocks_per_class = cls_len // Q_BLOCK

        n_steps = seq // (Q_BLOCK * Q_BLOCKS_PER_STEP)

        def blocks(step):
            for jj in range(Q_BLOCKS_PER_STEP):
                t = step * Q_BLOCKS_PER_STEP + jj
                row = pl.multiple_of(t * Q_BLOCK, Q_BLOCK)
                cls = t // blocks_per_class
                q0 = (t % blocks_per_class) * Q_BLOCK
                w0 = jnp.clip(q0 - HALF_BAND, 0, cls_len - win)
                start = pl.multiple_of(cls * cls_len + w0, BF16_SUBLANES)
                yield jj, row, cls, q0, w0, start

        def scores(step, par):
            for jj, row, cls, q0, w0, start in blocks(step):
                bias = bias_scr[(q0 - w0) // HALF_BAND, :, :win]
                q_ab = jnp.concatenate([src(0, row, Q_BLOCK), src(1, row, Q_BLOCK)], axis=0)
                s = lax.dot_general(q_ab, src(2, start, win), _TRANS_B,
                                    preferred_element_type=f32)
                s_scr[par, 2 * jj, :, :win] = s[:Q_BLOCK] + bias
                s_scr[par, 2 * jj + 1, :, :win] = s[Q_BLOCK:] + bias

        def softmax(par):
            for jj in range(Q_BLOCKS_PER_STEP):
                maxes = []
                for idx in (2 * jj, 2 * jj + 1):
                    s = s_scr[par, idx, :, :win]
                    m = jnp.max(s, axis=-1, keepdims=True)
                    p_scr[par, idx, :, :win] = jnp.exp2(s - m).astype(bf16)
                    maxes.append(m)
                m_scr[par, jj] = jnp.where(first_head, maxes[0], maxes[1])

        def values(step, par):
            for jj, row, cls, q0, w0, start in blocks(step):
                r_a = jnp.dot(p_scr[par, 2 * jj, :, :win], src(3, start, win),
                              preferred_element_type=f32)
                r_b = jnp.dot(p_scr[par, 2 * jj + 1, :, :win], src(4, start, win),
                              preferred_element_type=f32)
                o_un = jnp.where(first_head, r_a, r_b)
                den = pltpu.roll(jnp.where(first_head, r_b, r_a), HEAD_DIM, axis=1)
                toks = pl.ds(cls + dil * q0, Q_BLOCK, stride=dil) if dil > 1 else pl.ds(row, Q_BLOCK)
                fbuf[out_slot, toks, :] = o_un / den
                fbuf[out_slot + 1, toks, :] = m_scr[par, jj] + jnp.log2(den)

        assert n_steps % 2 == 0 and n_steps >= 2
        n_loop = (n_steps - 2) // 2
        per_part = n_copy // (n_loop + 2) if relayout else 0

        def fill(first, count):
            for i in range(count):
                relayout(first + i)

        scores(0, 0)
        softmax(0)
        scores(1, 1)
        fill(0, per_part)

        def body(j, carry):
            step = 2 * j
            values(step, 0)
            softmax(1)
            scores(step + 2, 0)
            fill((j + 1) * per_part, per_part)
            values(step + 1, 1)
            softmax(0)
            scores(step + 3, 1)
            return carry

        lax.fori_loop(0, n_loop, body, 0)
        values(n_steps - 2, 0)
        softmax(1)
        if relayout:
            fill((n_loop + 1) * per_part, n_copy - (n_loop + 1) * per_part)
        values(n_steps - 1, 1)

    run_group(lambda n, start, size: rows_bf16(inputs[n], (), start, size), 1, 4, split4)
    run_group(lambda n, start, size: rows_bf16(a16, (n,), start, size), 4, 2, split16)
    run_group(lambda n, start, size: rows_bf16(b16, (n,), start, size), 16, 0)

    def merge(c, carry):
        rows = pl.ds(pl.multiple_of(c * COPY_ROWS, COPY_ROWS), COPY_ROWS)
        lse = [fbuf[2 * g + 1, rows, :] for g in range(3)]
        top = jnp.maximum(jnp.maximum(lse[0], lse[1]), lse[2])
        w = [jnp.exp2(l - top) for l in lse]
        num = w[0] * fbuf[0, rows, :] + w[1] * fbuf[2, rows, :] + w[2] * fbuf[4, rows, :]
        out_ref[rows, :] = (num / (w[0] + w[1] + w[2])).astype(out_ref.dtype)
        return carry

    lax.fori_loop(0, seq // COPY_ROWS, merge, 0, unroll=COPY_UNROLL)


def _dil_attn(q, k, v, batch, seq):
    words = seq // 2
    pair = pl.BlockSpec((words, LANES), lambda b, c: (b, c))
    head_a = pl.BlockSpec((words, LANES), lambda b, c: (b, 2 * c))
    head_b = pl.BlockSpec((words, LANES), lambda b, c: (b, 2 * c + 1))
    n_tiles = 2 * Q_BLOCKS_PER_STEP
    win = Q_BLOCK + 2 * HALF_BAND
    return pl.pallas_call(
        functools.partial(_dil_attn_kernel, seq=seq),
        grid=(batch, ATTN_WIDTH // LANES),
        in_specs=[head_a, head_b, pair, head_a, head_b],
        out_specs=pl.BlockSpec((seq, LANES), lambda b, c: (b, c)),
        out_shape=jax.ShapeDtypeStruct((batch * seq, ATTN_WIDTH), jnp.bfloat16),
        scratch_shapes=[
            pltpu.VMEM((6, seq, LANES), jnp.float32),
            pltpu.VMEM((5, words, LANES), jnp.uint32),
            pltpu.VMEM((5, words, LANES), jnp.uint32),
            pltpu.VMEM((3, Q_BLOCK, win), jnp.float32),
            pltpu.VMEM((2, n_tiles, Q_BLOCK, win), jnp.float32),
            pltpu.VMEM((2, n_tiles, Q_BLOCK, win), jnp.bfloat16),
            pltpu.VMEM((2, Q_BLOCKS_PER_STEP, Q_BLOCK, LANES), jnp.float32),
        ],
        compiler_params=pltpu.CompilerParams(
            dimension_semantics=("arbitrary", "arbitrary"),
            vmem_limit_bytes=VMEM_LIMIT_BYTES),
        name="dil_attn",
    )(q, q, k, v, v)


def _tail_kernel(x_ref, p_ref, attn_ref, sga_ref, conv_ref, na_ref, wout_ref, nple_ref,
                 wgate_ref, bgate_ref, wple_ref, nfin_ref, y_ref):
    f32 = jnp.float32
    attn = attn_ref[...].astype(f32)
    gated_attn = attn * _rms_scale(attn) * na_ref[...] * sga_ref[...].astype(f32)
    merged = jnp.concatenate([gated_attn.astype(jnp.bfloat16), conv_ref[...]], axis=1)
    x1 = x_ref[...] + jnp.dot(merged, wout_ref[...], preferred_element_type=f32)

    r = (x1 * _rms_scale(x1) * nple_ref[...]).astype(jnp.bfloat16)
    gate = jax.nn.sigmoid(jnp.dot(r, wgate_ref[...], preferred_element_type=f32)
                          + bgate_ref[...])
    emb = jnp.dot(p_ref[...].astype(jnp.bfloat16), wple_ref[...],
                  preferred_element_type=f32)
    x2 = x1 + emb * gate
    y_ref[...] = x2 * _rms_scale(x2) * nfin_ref[...]


def _tail(x2, p2, attn, sga, conv, norm_attn, w_out, norm_ple, w_gate, b_gate, w_ple,
          norm_final, tm):
    T = x2.shape[0]
    row = lambda i: (i, 0)
    fixed = lambda i: (0, 0)
    a_spec = pl.BlockSpec((tm, ATTN_WIDTH), row)
    full = lambda a: pl.BlockSpec(a.shape, fixed)
    return pl.pallas_call(
        _tail_kernel,
        grid=(T // tm,),
        in_specs=[
            pl.BlockSpec((tm, D_MODEL), row),
            pl.BlockSpec((tm, PLE_DIM), row),
            a_spec, a_spec, a_spec,
            full(norm_attn), full(w_out), full(norm_ple), full(w_gate), full(b_gate),
            full(w_ple), full(norm_final),
        ],
        out_specs=pl.BlockSpec((tm, D_MODEL), row),
        out_shape=jax.ShapeDtypeStruct((T, D_MODEL), jnp.float32),
        compiler_params=pltpu.CompilerParams(
            dimension_semantics=("arbitrary",),
            vmem_limit_bytes=VMEM_LIMIT_BYTES),
        name="tail",
    )(x2, p2, attn, sga, conv, norm_attn, w_out, norm_ple, w_gate, b_gate, w_ple, norm_final)


def _rope_tables(seq):
    half = ROT_DIM // 2
    inv = jnp.power(jnp.float32(ROPE_THETA),
                    -jnp.arange(0, ROT_DIM, 2, dtype=jnp.float32) / ROT_DIM)
    ang = jnp.arange(seq, dtype=jnp.float32)[:, None] * inv[None, :]
    cos, sin = jnp.cos(ang), jnp.sin(ang)
    pad = HEAD_DIM - ROT_DIM
    ones = jnp.ones((seq, pad), jnp.float32)
    zeros = jnp.zeros((seq, pad), jnp.float32)
    zh = jnp.zeros((seq, half), jnp.float32)
    per_head = lambda parts: jnp.tile(jnp.concatenate(parts, axis=1), (1, LANES // HEAD_DIM))
    return (per_head([cos, cos, ones]),
            per_head([-sin, zh, zeros]),
            per_head([zh, sin, zeros]))


def _trunk(x, p, tables, norm_mix, w_in, conv_w, norm_attn, norm_conv, w_out,
           norm_ple, w_gate, b_gate, w_ple, norm_final):
    B, S, _ = x.shape
    assert S % (16 * Q_BLOCK) == 0
    x2 = x.reshape(B * S, D_MODEL)
    p2 = p.reshape(B * S, PLE_DIM)
    cos_t, sinlo_t, sinhi_t = tables
    q, k, v, sga, conv = _in_proj(x2, norm_mix, w_in, cos_t[:S], sinlo_t[:S], sinhi_t[:S],
                                  conv_w, norm_conv, S, tm=1024)
    attn = _dil_attn(q, k, v, B, S)
    y = _tail(x2, p2, attn, sga, conv, norm_attn, w_out, norm_ple, w_gate, b_gate, w_ple,
              norm_final, tm=1024)
    return y.reshape(B, S, D_MODEL)


def kernel(x_prompt, x_sample, p_prompt, p_sample, norm_mix, w_in, conv_w,
           norm_attn_out, norm_conv_out, w_out, norm_ple, w_ple_gate, b_ple_gate,
           w_ple_proj, norm_final):
    depth = w_in.shape[0]
    assert depth == 1, "single-layer trunk"
    bf16 = jnp.bfloat16
    tables = _rope_tables(max(x_prompt.shape[1], x_sample.shape[1]))
    params = (norm_mix[0][None], w_in[0].astype(bf16), conv_w[0],
              norm_attn_out[0][None], norm_conv_out[0][None], w_out[0].astype(bf16),
              norm_ple[0][None], w_ple_gate[0].astype(bf16), b_ple_gate[0][None],
              w_ple_proj[0].astype(bf16), norm_final[None])
    y_prompt = _trunk(x_prompt, p_prompt[0], tables, *params)
    y_sample = _trunk(x_sample, p_sample[0], tables, *params)
    return (y_prompt, y_sample)
```

```python
import functools

import jax
import jax.numpy as jnp
from jax import lax
from jax.experimental import pallas as pl
from jax.experimental.pallas import tpu as pltpu

D_MODEL = 1024
ATTN_WIDTH = 512
CONV_WIDTH = 512
HEAD_DIM = 64
ROT_DIM = 16
ROPE_THETA = 500000.0
HALF_BAND = 64
DILATIONS = (1, 4, 16)
PLE_DIM = 256
EPS = 1e-6
NEG = -1e30
LOG2_E = 1.4426950408889634

LANES = 128
F32_SUBLANES = 8
BF16_SUBLANES = 16
Q_BLOCK = 128
Q_BLOCKS_PER_STEP = 4
COPY_ROWS = 128
COPY_UNROLL = 8
VMEM_LIMIT_BYTES = 56 * 1024 * 1024

_TRANS_B = (((1,), (1,)), ((), ()))


def _rms_scale(v):
    return lax.rsqrt(jnp.mean(v * v, axis=-1, keepdims=True) + EPS)


def _in_proj_kernel(x_ref, xprev_ref, xnext_ref, g_ref, w_ref, cos_ref, sinlo_ref, sinhi_ref,
                    convw_ref, nconv_ref, q_ref, k_ref, v_ref, sga_ref, conv_ref,
                    *, tiles_per_seq):
    tm = x_ref.shape[0]

    def normed(x):
        return (x * _rms_scale(x) * g_ref[...]).astype(jnp.bfloat16)

    h = normed(x_ref[...])

    def cols(j, hh=None, n=1):
        return jnp.dot(h if hh is None else hh,
                       w_ref[:, j * ATTN_WIDTH:(j + n) * ATTN_WIDTH],
                       preferred_element_type=jnp.float32)

    cos = cos_ref[...]
    sinlo = sinlo_ref[...]
    sinhi = sinhi_ref[...]

    def rope(t):
        outs = []
        for c in range(ATTN_WIDTH // LANES):
            tc = t[:, c * LANES:(c + 1) * LANES]
            up = pltpu.roll(tc, LANES - ROT_DIM // 2, axis=1)
            dn = pltpu.roll(tc, ROT_DIM // 2, axis=1)
            outs.append(tc * cos + up * sinlo + dn * sinhi)
        return jnp.concatenate(outs, axis=1)

    first_head = lax.broadcasted_iota(jnp.int32, (1, LANES), 1) < HEAD_DIM

    def per_head(t, fill):
        outs = []
        for c in range(ATTN_WIDTH // LANES):
            tc = t[:, c * LANES:(c + 1) * LANES]
            outs += [jnp.where(first_head, tc, fill), jnp.where(first_head, fill, tc)]
        return jnp.concatenate(outs, axis=1).astype(jnp.bfloat16)

    def words(t):
        return pltpu.bitcast(t, jnp.uint32)

    halo = normed(jnp.concatenate([xprev_ref[...], xnext_ref[...]], axis=0))
    cc_ch = cols(5, jnp.concatenate([h, halo], axis=0), 2)
    u_ext = cc_ch[:, :CONV_WIDTH] * cc_ch[:, CONV_WIDTH:]
    u = u_ext[:tm]
    t_in_seq = pl.program_id(0) % tiles_per_seq
    prev_row = u_ext[tm + F32_SUBLANES - 1:tm + F32_SUBLANES, :]
    next_row = u_ext[tm + F32_SUBLANES:tm + F32_SUBLANES + 1, :]
    prev_row = jnp.where(t_in_seq == 0, 0.0, prev_row)
    next_row = jnp.where(t_in_seq == tiles_per_seq - 1, 0.0, next_row)

    q_ref[...] = words(per_head(rope(cols(0)) * (HEAD_DIM ** -0.5 * LOG2_E), 0.0))

    ridx = lax.broadcasted_iota(jnp.int32, (tm, 1), 0)
    u_before = jnp.where(ridx == 0, prev_row, pltpu.roll(u, 1, axis=0))
    u_after = jnp.where(ridx == tm - 1, next_row, pltpu.roll(u, tm - 1, axis=0))
    cw = convw_ref[...]
    taps = cw[0:1] * u_before + cw[1:2] * u + cw[2:3] * u_after

    k_ref[...] = words(rope(cols(1)).astype(jnp.bfloat16))

    conv = cols(4) * taps
    conv = conv * _rms_scale(conv) * nconv_ref[...]

    v_ref[...] = words(per_head(cols(2), 1.0))

    conv_ref[...] = (conv * jax.nn.silu(cols(7))).astype(jnp.bfloat16)
    sga_ref[...] = jax.nn.silu(cols(3)).astype(jnp.bfloat16)


def _in_proj(x2, g, w_bf16, cos_t, sinlo_t, sinhi_t, conv_w, norm_conv, seq, tm):
    T = x2.shape[0]
    nseq = seq // tm
    halo_blocks = T // F32_SUBLANES
    per_tile = tm // F32_SUBLANES
    row = lambda i: (i, 0)
    fixed = lambda i: (0, 0)
    pos = lambda i: (i % nseq, 0)
    prev = lambda i: (jnp.maximum(i * per_tile - 1, 0), 0)
    nxt = lambda i: (jnp.minimum((i + 1) * per_tile, halo_blocks - 1), 0)
    out = jax.ShapeDtypeStruct((T, ATTN_WIDTH), jnp.bfloat16)
    wout = jax.ShapeDtypeStruct((T // 2, ATTN_WIDTH), jnp.uint32)
    wout2 = jax.ShapeDtypeStruct((T // 2, 2 * ATTN_WIDTH), jnp.uint32)
    spec = pl.BlockSpec((tm, ATTN_WIDTH), row)
    wspec = pl.BlockSpec((tm // 2, ATTN_WIDTH), row)
    wspec2 = pl.BlockSpec((tm // 2, 2 * ATTN_WIDTH), row)
    full = lambda a: pl.BlockSpec(a.shape, fixed)
    return pl.pallas_call(
        functools.partial(_in_proj_kernel, tiles_per_seq=nseq),
        grid=(T // tm,),
        in_specs=[
            pl.BlockSpec((tm, D_MODEL), row),
            pl.BlockSpec((F32_SUBLANES, D_MODEL), prev),
            pl.BlockSpec((F32_SUBLANES, D_MODEL), nxt),
            full(g), full(w_bf16),
            pl.BlockSpec((tm, LANES), pos),
            pl.BlockSpec((tm, LANES), pos),
            pl.BlockSpec((tm, LANES), pos),
            full(conv_w), full(norm_conv),
        ],
        out_specs=[wspec2, wspec, wspec2, spec, spec],
        out_shape=[wout2, wout, wout2, out, out],
        compiler_params=pltpu.CompilerParams(
            dimension_semantics=("arbitrary",),
            vmem_limit_bytes=VMEM_LIMIT_BYTES),
        name="in_proj",
    )(x2, x2, x2, g, w_bf16, cos_t, sinlo_t, sinhi_t, conv_w, norm_conv)


def _dil_attn_kernel(qa_ref, qb_ref, k_ref, va_ref, vb_ref, out_ref, fbuf, a16, b16, bias_scr,
                     s_scr, p_scr, m_scr, *, seq):
    f32, bf16, u32 = jnp.float32, jnp.bfloat16, jnp.uint32
    len4, len16 = seq // 4, seq // 16
    first_head = lax.broadcasted_iota(jnp.int32, (1, LANES), 1) < HEAD_DIM
    inputs = (qa_ref, qb_ref, k_ref, va_ref, vb_ref)
    n_copy = seq // (2 * COPY_ROWS)

    def rows_bf16(ref, lead, start, size):
        words = pl.ds(pl.multiple_of(start // 2, BF16_SUBLANES // 2), size // 2)
        return pltpu.bitcast(ref[(*lead, words, slice(None))], bf16)

    def split_row_pairs(words, lead, word0, dst, x, row_even, row_odd):
        w = words[(*lead, pl.ds(word0, COPY_ROWS, stride=2), slice(None))]
        even, odd = (pltpu.unpack_elementwise(w, index=i, packed_dtype=bf16,
                                              unpacked_dtype=f32).astype(bf16) for i in (0, 1))
        half = COPY_ROWS // 2
        dst[x, pl.ds(pl.multiple_of(row_even // 2, half), half), :] = pltpu.bitcast(even, u32)
        dst[x, pl.ds(pl.multiple_of(row_odd // 2, half), half), :] = pltpu.bitcast(odd, u32)

    def split4(c):
        chunks = len4 // COPY_ROWS
        pair = c // chunks
        u0 = (c % chunks) * COPY_ROWS
        for x in range(5):
            split_row_pairs(inputs[x], (), pair + 2 * u0, a16, x,
                            2 * pair * len4 + u0, (2 * pair + 1) * len4 + u0)

    def split16(c):
        chunks = len16 // COPY_ROWS
        w0 = (c % chunks) * COPY_ROWS
        pair = (c // chunks) % 2
        r4 = c // (2 * chunks)
        for x in range(5):
            split_row_pairs(a16, (x,), r4 * (len4 // 2) + pair + 2 * w0,
                            b16, x, (r4 + 8 * pair) * len16 + w0,
                            (r4 + 8 * pair + 4) * len16 + w0)

    win_max = Q_BLOCK + 2 * HALF_BAND
    rel = (lax.broadcasted_iota(jnp.int32, (Q_BLOCK, win_max), 0)
           - lax.broadcasted_iota(jnp.int32, (Q_BLOCK, win_max), 1))
    for n in range(3):
        bias_scr[n] = jnp.where(jnp.abs(rel + n * HALF_BAND) <= HALF_BAND, 0.0, NEG)

    def run_group(src, dil, out_slot, relayout=None):
        cls_len = seq // dil
        win = min(cls_len, win_max)
        blocks_per_class = cls_len // Q_BLOCK

        n_steps = seq // (Q_BLOCK * Q_BLOCKS_PER_STEP)

        def blocks(step):
            for jj in range(Q_BLOCKS_PER_STEP):
                t = step * Q_BLOCKS_PER_STEP + jj
                row = pl.multiple_of(t * Q_BLOCK, Q_BLOCK)
                cls = t // blocks_per_class
                q0 = (t % blocks_per_class) * Q_BLOCK
                w0 = jnp.clip(q0 - HALF_BAND, 0, cls_len - win)
                start = pl.multiple_of(cls * cls_len + w0, BF16_SUBLANES)
                yield jj, row, cls, q0, w0, start

        def scores(step, par):
            for jj, row, cls, q0, w0, start in blocks(step):
                bias = bias_scr[(q0 - w0) // HALF_BAND, :, :win]
                q_ab = jnp.concatenate([src(0, row, Q_BLOCK), src(1, row, Q_BLOCK)], axis=0)
                s = lax.dot_general(q_ab, src(2, start, win), _TRANS_B,
                                    preferred_element_type=f32)
                s_scr[par, 2 * jj, :, :win] = s[:Q_BLOCK] + bias
                s_scr[par, 2 * jj + 1, :, :win] = s[Q_BLOCK:] + bias

        def softmax(par):
            for jj in range(Q_BLOCKS_PER_STEP):
                maxes = []
                for idx in (2 * jj, 2 * jj + 1):
                    s = s_scr[par, idx, :, :win]
                    m = jnp.max(s, axis=-1, keepdims=True)
                    p_scr[par, idx, :, :win] = jnp.exp2(s - m).astype(bf16)
                    maxes.append(m)
                m_scr[par, jj] = jnp.where(first_head, maxes[0], maxes[1])

        def values(step, par):
            for jj, row, cls, q0, w0, start in blocks(step):
                r_a = jnp.dot(p_scr[par, 2 * jj, :, :win], src(3, start, win),
                              preferred_element_type=f32)
                r_b = jnp.dot(p_scr[par, 2 * jj + 1, :, :win], src(4, start, win),
                              preferred_element_type=f32)
                o_un = jnp.where(first_head, r_a, r_b)
                den = pltpu.roll(jnp.where(first_head, r_b, r_a), HEAD_DIM, axis=1)
                toks = pl.ds(cls + dil * q0, Q_BLOCK, stride=dil) if dil > 1 else pl.ds(row, Q_BLOCK)
                fbuf[out_slot, toks, :] = o_un / den
                fbuf[out_slot + 1, toks, :] = m_scr[par, jj] + jnp.log2(den)

        assert n_steps % 2 == 0 and n_steps >= 2
        n_loop = (n_steps - 2) // 2
        per_part = n_copy // (n_loop + 2) if relayout else 0

        def fill(first, count):
            for i in range(count):
                relayout(first + i)

        scores(0, 0)
        softmax(0)
        scores(1, 1)
        fill(0, per_part)

        def body(j, carry):
            step = 2 * j
            values(step, 0)
            softmax(1)
            scores(step + 2, 0)
            fill((j + 1) * per_part, per_part)
            values(step + 1, 1)
            softmax(0)
            scores(step + 3, 1)
            return carry

        lax.fori_loop(0, n_loop, body, 0)
        values(n_steps - 2, 0)
        softmax(1)
        if relayout:
            fill((n_loop + 1) * per_part, n_copy - (n_loop + 1) * per_part)
        values(n_steps - 1, 1)

    run_group(lambda n, start, size: rows_bf16(inputs[n], (), start, size), 1, 4, split4)
    run_group(lambda n, start, size: rows_bf16(a16, (n,), start, size), 4, 2, split16)
    run_group(lambda n, start, size: rows_bf16(b16, (n,), start, size), 16, 0)

    def merge(c, carry):
        rows = pl.ds(pl.multiple_of(c * COPY_ROWS, COPY_ROWS), COPY_ROWS)
        lse = [fbuf[2 * g + 1, rows, :] for g in range(3)]
        top = jnp.maximum(jnp.maximum(lse[0], lse[1]), lse[2])
        w = [jnp.exp2(l - top) for l in lse]
        num = w[0] * fbuf[0, rows, :] + w[1] * fbuf[2, rows, :] + w[2] * fbuf[4, rows, :]
        out_ref[rows, :] = (num / (w[0] + w[1] + w[2])).astype(out_ref.dtype)
        return carry

    lax.fori_loop(0, seq // COPY_ROWS, merge, 0, unroll=COPY_UNROLL)


def _dil_attn(q, k, v, batch, seq):
    words = seq // 2
    pair = pl.BlockSpec((words, LANES), lambda b, c: (b, c))
    head_a = pl.BlockSpec((words, LANES), lambda b, c: (b, 2 * c))
    head_b = pl.BlockSpec((words, LANES), lambda b, c: (b, 2 * c + 1))
    n_tiles = 2 * Q_BLOCKS_PER_STEP
    win = Q_BLOCK + 2 * HALF_BAND
    return pl.pallas_call(
        functools.partial(_dil_attn_kernel, seq=seq),
        grid=(batch, ATTN_WIDTH // LANES),
        in_specs=[head_a, head_b, pair, head_a, head_b],
        out_specs=pl.BlockSpec((seq, LANES), lambda b, c: (b, c)),
        out_shape=jax.ShapeDtypeStruct((batch * seq, ATTN_WIDTH), jnp.bfloat16),
        scratch_shapes=[
            pltpu.VMEM((6, seq, LANES), jnp.float32),
            pltpu.VMEM((5, words, LANES), jnp.uint32),
            pltpu.VMEM((5, words, LANES), jnp.uint32),
            pltpu.VMEM((3, Q_BLOCK, win), jnp.float32),
            pltpu.VMEM((2, n_tiles, Q_BLOCK, win), jnp.float32),
            pltpu.VMEM((2, n_tiles, Q_BLOCK, win), jnp.bfloat16),
            pltpu.VMEM((2, Q_BLOCKS_PER_STEP, Q_BLOCK, LANES), jnp.float32),
        ],
        compiler_params=pltpu.CompilerParams(
            dimension_semantics=("arbitrary", "arbitrary"),
            vmem_limit_bytes=VMEM_LIMIT_BYTES),
        name="dil_attn",
    )(q, q, k, v, v)


def _tail_kernel(x_ref, p_ref, attn_ref, sga_ref, conv_ref, na_ref, wout_ref, nple_ref,
                 wgate_ref, bgate_ref, wple_ref, nfin_ref, y_ref):
    f32 = jnp.float32
    attn = attn_ref[...].astype(f32)
    gated_attn = attn * _rms_scale(attn) * na_ref[...] * sga_ref[...].astype(f32)
    merged = jnp.concatenate([gated_attn.astype(jnp.bfloat16), conv_ref[...]], axis=1)
    x1 = x_ref[...] + jnp.dot(merged, wout_ref[...], preferred_element_type=f32)

    r = (x1 * _rms_scale(x1) * nple_ref[...]).astype(jnp.bfloat16)
    gate = jax.nn.sigmoid(jnp.dot(r, wgate_ref[...], preferred_element_type=f32)
                          + bgate_ref[...])
    emb = jnp.dot(p_ref[...].astype(jnp.bfloat16), wple_ref[...],
                  preferred_element_type=f32)
    x2 = x1 + emb * gate
    y_ref[...] = x2 * _rms_scale(x2) * nfin_ref[...]


def _tail(x2, p2, attn, sga, conv, norm_attn, w_out, norm_ple, w_gate, b_gate, w_ple,
          norm_final, tm):
    T = x2.shape[0]
    row = lambda i: (i, 0)
    fixed = lambda i: (0, 0)
    a_spec = pl.BlockSpec((tm, ATTN_WIDTH), row)
    full = lambda a: pl.BlockSpec(a.shape, fixed)
    return pl.pallas_call(
        _tail_kernel,
        grid=(T // tm,),
        in_specs=[
            pl.BlockSpec((tm, D_MODEL), row),
            pl.BlockSpec((tm, PLE_DIM), row),
            a_spec, a_spec, a_spec,
            full(norm_attn), full(w_out), full(norm_ple), full(w_gate), full(b_gate),
            full(w_ple), full(norm_final),
        ],
        out_specs=pl.BlockSpec((tm, D_MODEL), row),
        out_shape=jax.ShapeDtypeStruct((T, D_MODEL), jnp.float32),
        compiler_params=pltpu.CompilerParams(
            dimension_semantics=("arbitrary",),
            vmem_limit_bytes=VMEM_LIMIT_BYTES),
        name="tail",
    )(x2, p2, attn, sga, conv, norm_attn, w_out, norm_ple, w_gate, b_gate, w_ple, norm_final)


def _rope_tables(seq):
    half = ROT_DIM // 2
    inv = jnp.power(jnp.float32(ROPE_THETA),
                    -jnp.arange(0, ROT_DIM, 2, dtype=jnp.float32) / ROT_DIM)
    ang = jnp.arange(seq, dtype=jnp.float32)[:, None] * inv[None, :]
    cos, sin = jnp.cos(ang), jnp.sin(ang)
    pad = HEAD_DIM - ROT_DIM
    ones = jnp.ones((seq, pad), jnp.float32)
    zeros = jnp.zeros((seq, pad), jnp.float32)
    zh = jnp.zeros((seq, half), jnp.float32)
    per_head = lambda parts: jnp.tile(jnp.concatenate(parts, axis=1), (1, LANES // HEAD_DIM))
    return (per_head([cos, cos, ones]),
            per_head([-sin, zh, zeros]),
            per_head([zh, sin, zeros]))


def _trunk(x, p, tables, norm_mix, w_in, conv_w, norm_attn, norm_conv, w_out,
           norm_ple, w_gate, b_gate, w_ple, norm_final):
    B, S, _ = x.shape
    assert S % (16 * Q_BLOCK) == 0
    x2 = x.reshape(B * S, D_MODEL)
    p2 = p.reshape(B * S, PLE_DIM)
    cos_t, sinlo_t, sinhi_t = tables
    q, k, v, sga, conv = _in_proj(x2, norm_mix, w_in, cos_t[:S], sinlo_t[:S], sinhi_t[:S],
                                  conv_w, norm_conv, S, tm=1024)
    attn = _dil_attn(q, k, v, B, S)
    y = _tail(x2, p2, attn, sga, conv, norm_attn, w_out, norm_ple, w_gate, b_gate, w_ple,
              norm_final, tm=1024)
    return y.reshape(B, S, D_MODEL)


def kernel(x_prompt, x_sample, p_prompt, p_sample, norm_mix, w_in, conv_w,
           norm_attn_out, norm_conv_out, w_out, norm_ple, w_ple_gate, b_ple_gate,
           w_ple_proj, norm_final):
    depth = w_in.shape[0]
    assert depth == 1, "single-layer trunk"
    bf16 = jnp.bfloat16
    tables = _rope_tables(max(x_prompt.shape[1], x_sample.shape[1]))
    params = (norm_mix[0][None], w_in[0].astype(bf16), conv_w[0],
              norm_attn_out[0][None], norm_conv_out[0][None], w_out[0].astype(bf16),
              norm_ple[0][None], w_ple_gate[0].astype(bf16), b_ple_gate[0][None],
              w_ple_proj[0].astype(bf16), norm_final[None])
    y_prompt = _trunk(x_prompt, p_prompt[0], tables, *params)
    y_sample = _trunk(x_sample, p_sample[0], tables, *params)
    return (y_prompt, y_sample)
```

```python
import functools

import jax
import jax.numpy as jnp
from jax import lax
from jax.experimental import pallas as pl
from jax.experimental.pallas import tpu as pltpu

D_MODEL = 1024
ATTN_WIDTH = 512
CONV_WIDTH = 512
HEAD_DIM = 64
ROT_DIM = 16
ROPE_THETA = 500000.0
HALF_BAND = 64
PLE_DIM = 256
EPS = 1e-6
NEG = -1e30
LOG2_E = 1.4426950408889634

LANES = 128
F32_SUBLANES = 8
BF16_SUBLANES = 16
Q_BLOCK = 128
Q_BLOCKS_PER_STEP = 4
COPY_ROWS = 128
MERGE_UNROLL = 8
VMEM_LIMIT_BYTES = 56 * 1024 * 1024

_TRANS_B = (((1,), (1,)), ((), ()))


def _rms_scale(v):
    return lax.rsqrt(jnp.mean(v * v, axis=-1, keepdims=True) + EPS)


def _in_proj_kernel(x_ref, xprev_ref, xnext_ref, g_ref, w_ref, cos_ref, sinlo_ref, sinhi_ref,
                    convw_ref, nconv_ref, q_ref, k_ref, v_ref, sga_ref, conv_ref,
                    *, tiles_per_seq):
    tm = x_ref.shape[0]

    def normed(x):
        return (x * _rms_scale(x) * g_ref[...]).astype(jnp.bfloat16)

    h = normed(x_ref[...])

    def cols(j, hh=None, n=1):
        return jnp.dot(h if hh is None else hh,
                       w_ref[:, j * ATTN_WIDTH:(j + n) * ATTN_WIDTH],
                       preferred_element_type=jnp.float32)

    cos = cos_ref[...]
    sinlo = sinlo_ref[...]
    sinhi = sinhi_ref[...]

    def rope(t):
        outs = []
        for c in range(ATTN_WIDTH // LANES):
            tc = t[:, c * LANES:(c + 1) * LANES]
            up = pltpu.roll(tc, LANES - ROT_DIM // 2, axis=1)
            dn = pltpu.roll(tc, ROT_DIM // 2, axis=1)
            outs.append(tc * cos + up * sinlo + dn * sinhi)
        return jnp.concatenate(outs, axis=1)

    first_head = lax.broadcasted_iota(jnp.int32, (1, LANES), 1) < HEAD_DIM

    def per_head(t, fill):
        outs = []
        for c in range(ATTN_WIDTH // LANES):
            tc = t[:, c * LANES:(c + 1) * LANES]
            outs += [jnp.where(first_head, tc, fill), jnp.where(first_head, fill, tc)]
        return jnp.concatenate(outs, axis=1).astype(jnp.bfloat16)

    def words(t):
        return pltpu.bitcast(t, jnp.uint32)

    halo = normed(jnp.concatenate([xprev_ref[...], xnext_ref[...]], axis=0))
    cc_ch = cols(5, jnp.concatenate([h, halo], axis=0), 2)
    u_ext = cc_ch[:, :CONV_WIDTH] * cc_ch[:, CONV_WIDTH:]
    u = u_ext[:tm]
    t_in_seq = pl.program_id(0) % tiles_per_seq
    prev_row = u_ext[tm + F32_SUBLANES - 1:tm + F32_SUBLANES, :]
    next_row = u_ext[tm + F32_SUBLANES:tm + F32_SUBLANES + 1, :]
    prev_row = jnp.where(t_in_seq == 0, 0.0, prev_row)
    next_row = jnp.where(t_in_seq == tiles_per_seq - 1, 0.0, next_row)

    q_ref[...] = words(per_head(rope(cols(0)) * (HEAD_DIM ** -0.5 * LOG2_E), 0.0))

    ridx = lax.broadcasted_iota(jnp.int32, (tm, 1), 0)
    u_before = jnp.where(ridx == 0, prev_row, pltpu.roll(u, 1, axis=0))
    u_after = jnp.where(ridx == tm - 1, next_row, pltpu.roll(u, tm - 1, axis=0))
    cw = convw_ref[...]
    taps = cw[0:1] * u_before + cw[1:2] * u + cw[2:3] * u_after

    k_ref[...] = words(rope(cols(1)).astype(jnp.bfloat16))

    conv = cols(4) * taps
    conv = conv * _rms_scale(conv) * nconv_ref[...]

    v_ref[...] = words(per_head(cols(2), 1.0))

    conv_ref[...] = (conv * jax.nn.silu(cols(7))).astype(jnp.bfloat16)
    sga_ref[...] = jax.nn.silu(cols(3)).astype(jnp.bfloat16)


def _in_proj(x2, g, w_bf16, cos_t, sinlo_t, sinhi_t, conv_w, norm_conv, seq, tm):
    T = x2.shape[0]
    nseq = seq // tm
    halo_blocks = T // F32_SUBLANES
    per_tile = tm // F32_SUBLANES
    row = lambda i: (i, 0)
    fixed = lambda i: (0, 0)
    pos = lambda i: (i % nseq, 0)
    prev = lambda i: (jnp.maximum(i * per_tile - 1, 0), 0)
    nxt = lambda i: (jnp.minimum((i + 1) * per_tile, halo_blocks - 1), 0)
    out = jax.ShapeDtypeStruct((T, ATTN_WIDTH), jnp.bfloat16)
    wout = jax.ShapeDtypeStruct((T // 2, ATTN_WIDTH), jnp.uint32)
    wout2 = jax.ShapeDtypeStruct((T // 2, 2 * ATTN_WIDTH), jnp.uint32)
    spec = pl.BlockSpec((tm, ATTN_WIDTH), row)
    wspec = pl.BlockSpec((tm // 2, ATTN_WIDTH), row)
    wspec2 = pl.BlockSpec((tm // 2, 2 * ATTN_WIDTH), row)
    full = lambda a: pl.BlockSpec(a.shape, fixed)
    return pl.pallas_call(
        functools.partial(_in_proj_kernel, tiles_per_seq=nseq),
        grid=(T // tm,),
        in_specs=[
            pl.BlockSpec((tm, D_MODEL), row),
            pl.BlockSpec((F32_SUBLANES, D_MODEL), prev),
            pl.BlockSpec((F32_SUBLANES, D_MODEL), nxt),
            full(g), full(w_bf16),
            pl.BlockSpec((tm, LANES), pos),
            pl.BlockSpec((tm, LANES), pos),
            pl.BlockSpec((tm, LANES), pos),
            full(conv_w), full(norm_conv),
        ],
        out_specs=[wspec2, wspec, wspec2, spec, spec],
        out_shape=[wout2, wout, wout2, out, out],
        compiler_params=pltpu.CompilerParams(
            dimension_semantics=("arbitrary",),
            vmem_limit_bytes=VMEM_LIMIT_BYTES),
        name="in_proj",
    )(x2, x2, x2, g, w_bf16, cos_t, sinlo_t, sinhi_t, conv_w, norm_conv)


def _dil_attn_kernel(qa_ref, qb_ref, k_ref, va_ref, vb_ref, out_ref, fbuf, a16, b16, bias_scr,
                     s_scr, p_scr, m_scr, *, seq):
    f32, bf16, u32 = jnp.float32, jnp.bfloat16, jnp.uint32
    len4, len16 = seq // 4, seq // 16
    first_head = lax.broadcasted_iota(jnp.int32, (1, LANES), 1) < HEAD_DIM
    inputs = (qa_ref, qb_ref, k_ref, va_ref, vb_ref)
    n_copy = seq // (2 * COPY_ROWS)

    def rows_bf16(ref, lead, start, size):
        words = pl.ds(pl.multiple_of(start // 2, BF16_SUBLANES // 2), size // 2)
        return pltpu.bitcast(ref[(*lead, words, slice(None))], bf16)

    def split_row_pairs(words, lead, word0, dst, x, row_even, row_odd):
        w = words[(*lead, pl.ds(word0, COPY_ROWS, stride=2), slice(None))]
        even, odd = (pltpu.unpack_elementwise(w, index=i, packed_dtype=bf16,
                                              unpacked_dtype=f32).astype(bf16) for i in (0, 1))
        half = COPY_ROWS // 2
        dst[x, pl.ds(pl.multiple_of(row_even // 2, half), half), :] = pltpu.bitcast(even, u32)
        dst[x, pl.ds(pl.multiple_of(row_odd // 2, half), half), :] = pltpu.bitcast(odd, u32)

    def split4(c):
        chunks = len4 // COPY_ROWS
        pair = c // chunks
        u0 = (c % chunks) * COPY_ROWS
        for x in range(5):
            split_row_pairs(inputs[x], (), pair + 2 * u0, a16, x,
                            2 * pair * len4 + u0, (2 * pair + 1) * len4 + u0)

    def split16(c):
        chunks = len16 // COPY_ROWS
        w0 = (c % chunks) * COPY_ROWS
        pair = (c // chunks) % 2
        r4 = c // (2 * chunks)
        for x in range(5):
            split_row_pairs(a16, (x,), r4 * (len4 // 2) + pair + 2 * w0,
                            b16, x, (r4 + 8 * pair) * len16 + w0,
                            (r4 + 8 * pair + 4) * len16 + w0)

    win_max = Q_BLOCK + 2 * HALF_BAND
    rel = (lax.broadcasted_iota(jnp.int32, (Q_BLOCK, win_max), 0)
           - lax.broadcasted_iota(jnp.int32, (Q_BLOCK, win_max), 1))
    for n in range(3):
        bias_scr[n] = jnp.where(jnp.abs(rel + n * HALF_BAND) <= HALF_BAND, 0.0, NEG)

    def run_group(src, dil, out_slot, relayout=None):
        cls_len = seq // dil
        win = min(cls_len, win_max)
        blocks_per_class = cls_len // Q_BLOCK

        n_steps = seq // (Q_BLOCK * Q_BLOCKS_PER_STEP)

        def blocks(step):
            for jj in range(Q_BLOCKS_PER_STEP):
                t = step * Q_BLOCKS_PER_STEP + jj
                row = pl.multiple_of(t * Q_BLOCK, Q_BLOCK)
                cls = t // blocks_per_class
                q0 = (t % blocks_per_class) * Q_BLOCK
                w0 = jnp.clip(q0 - HALF_BAND, 0, cls_len - win)
                start = pl.multiple_of(cls * cls_len + w0, BF16_SUBLANES)
                yield jj, row, cls, q0, w0, start

        def scores(step, par):
            for jj, row, cls, q0, w0, start in blocks(step):
                bias = bias_scr[(q0 - w0) // HALF_BAND, :, :win]
                q_ab = jnp.concatenate([src(0, row, Q_BLOCK), src(1, row, Q_BLOCK)], axis=0)
                s = lax.dot_general(q_ab, src(2, start, win), _TRANS_B,
                                    preferred_element_type=f32)
                s_scr[par, 2 * jj, :, :win] = s[:Q_BLOCK] + bias
                s_scr[par, 2 * jj + 1, :, :win] = s[Q_BLOCK:] + bias

        def softmax(par):
            for jj in range(Q_BLOCKS_PER_STEP):
                maxes = []
                for idx in (2 * jj, 2 * jj + 1):
                    s = s_scr[par, idx, :, :win]
                    m = jnp.max(s, axis=-1, keepdims=True)
                    p_scr[par, idx, :, :win] = jnp.exp2(s - m).astype(bf16)
                    maxes.append(m)
                m_scr[par, jj] = jnp.where(first_head, maxes[0], maxes[1])

        def values(step, par):
            for jj, row, cls, q0, w0, start in blocks(step):
                r_a = jnp.dot(p_scr[par, 2 * jj, :, :win], src(3, start, win),
                              preferred_element_type=f32)
                r_b = jnp.dot(p_scr[par, 2 * jj + 1, :, :win], src(4, start, win),
                              preferred_element_type=f32)
                o_un = jnp.where(first_head, r_a, r_b)
                den = pltpu.roll(jnp.where(first_head, r_b, r_a), HEAD_DIM, axis=1)
                toks = pl.ds(cls + dil * q0, Q_BLOCK, stride=dil) if dil > 1 else pl.ds(row, Q_BLOCK)
                fbuf[out_slot, toks, :] = o_un / den
                fbuf[out_slot + 1, toks, :] = m_scr[par, jj] + jnp.log2(den)

        assert n_steps % 2 == 0 and n_steps >= 2
        n_loop = (n_steps - 2) // 2
        per_part = n_copy // (n_loop + 2) if relayout else 0

        def fill(first, count):
            for i in range(count):
                relayout(first + i)

        scores(0, 0)
        softmax(0)
        scores(1, 1)
        fill(0, per_part)

        def body(j, carry):
            step = 2 * j
            values(step, 0)
            softmax(1)
            scores(step + 2, 0)
            fill((j + 1) * per_part, per_part)
            values(step + 1, 1)
            softmax(0)
            scores(step + 3, 1)
            return carry

        lax.fori_loop(0, n_loop, body, 0)
        values(n_steps - 2, 0)
        softmax(1)
        if relayout:
            fill((n_loop + 1) * per_part, n_copy - (n_loop + 1) * per_part)
        values(n_steps - 1, 1)

    run_group(lambda n, start, size: rows_bf16(inputs[n], (), start, size), 1, 4, split4)
    run_group(lambda n, start, size: rows_bf16(a16, (n,), start, size), 4, 2, split16)
    run_group(lambda n, start, size: rows_bf16(b16, (n,), start, size), 16, 0)

    def merge(c, carry):
        rows = pl.ds(pl.multiple_of(c * COPY_ROWS, COPY_ROWS), COPY_ROWS)
        lse = [fbuf[2 * g + 1, rows, :] for g in range(3)]
        top = jnp.maximum(jnp.maximum(lse[0], lse[1]), lse[2])
        w = [jnp.exp2(l - top) for l in lse]
        num = w[0] * fbuf[0, rows, :] + w[1] * fbuf[2, rows, :] + w[2] * fbuf[4, rows, :]
        out_ref[rows, :] = (num / (w[0] + w[1] + w[2])).astype(out_ref.dtype)
        return carry

    lax.fori_loop(0, seq // COPY_ROWS, merge, 0, unroll=MERGE_UNROLL)


def _dil_attn(q, k, v, batch, seq):
    words = seq // 2
    pair = pl.BlockSpec((words, LANES), lambda b, c: (b, c))
    head_a = pl.BlockSpec((words, LANES), lambda b, c: (b, 2 * c))
    head_b = pl.BlockSpec((words, LANES), lambda b, c: (b, 2 * c + 1))
    n_tiles = 2 * Q_BLOCKS_PER_STEP
    win = Q_BLOCK + 2 * HALF_BAND
    return pl.pallas_call(
        functools.partial(_dil_attn_kernel, seq=seq),
        grid=(batch, ATTN_WIDTH // LANES),
        in_specs=[head_a, head_b, pair, head_a, head_b],
        out_specs=pl.BlockSpec((seq, LANES), lambda b, c: (b, c)),
        out_shape=jax.ShapeDtypeStruct((batch * seq, ATTN_WIDTH), jnp.bfloat16),
        scratch_shapes=[
            pltpu.VMEM((6, seq, LANES), jnp.float32),
            pltpu.VMEM((5, words, LANES), jnp.uint32),
            pltpu.VMEM((5, words, LANES), jnp.uint32),
            pltpu.VMEM((3, Q_BLOCK, win), jnp.float32),
            pltpu.VMEM((2, n_tiles, Q_BLOCK, win), jnp.float32),
            pltpu.VMEM((2, n_tiles, Q_BLOCK, win), jnp.bfloat16),
            pltpu.VMEM((2, Q_BLOCKS_PER_STEP, Q_BLOCK, LANES), jnp.float32),
        ],
        compiler_params=pltpu.CompilerParams(
            dimension_semantics=("arbitrary", "arbitrary"),
            vmem_limit_bytes=VMEM_LIMIT_BYTES),
        name="dil_attn",
    )(q, q, k, v, v)


def _tail_kernel(x_ref, p_ref, attn_ref, sga_ref, conv_ref, na_ref, wout_ref, nple_ref,
                 wgate_ref, bgate_ref, wple_ref, nfin_ref, y_ref):
    f32 = jnp.float32
    attn = attn_ref[...].astype(f32)
    gated_attn = attn * _rms_scale(attn) * na_ref[...] * sga_ref[...].astype(f32)
    merged = jnp.concatenate([gated_attn.astype(jnp.bfloat16), conv_ref[...]], axis=1)
    x1 = x_ref[...] + jnp.dot(merged, wout_ref[...], preferred_element_type=f32)

    r = (x1 * _rms_scale(x1) * nple_ref[...]).astype(jnp.bfloat16)
    gate = jax.nn.sigmoid(jnp.dot(r, wgate_ref[...], preferred_element_type=f32)
                          + bgate_ref[...])
    emb = jnp.dot(p_ref[...].astype(jnp.bfloat16), wple_ref[...],
                  preferred_element_type=f32)
    x2 = x1 + emb * gate
    y_ref[...] = x2 * _rms_scale(x2) * nfin_ref[...]


def _tail(x2, p2, attn, sga, conv, norm_attn, w_out, norm_ple, w_gate, b_gate, w_ple,
          norm_final, tm):
    T = x2.shape[0]
    row = lambda i: (i, 0)
    fixed = lambda i: (0, 0)
    a_spec = pl.BlockSpec((tm, ATTN_WIDTH), row)
    full = lambda a: pl.BlockSpec(a.shape, fixed)
    return pl.pallas_call(
        _tail_kernel,
        grid=(T // tm,),
        in_specs=[
            pl.BlockSpec((tm, D_MODEL), row),
            pl.BlockSpec((tm, PLE_DIM), row),
            a_spec, a_spec, a_spec,
            full(norm_attn), full(w_out), full(norm_ple), full(w_gate), full(b_gate),
            full(w_ple), full(norm_final),
        ],
        out_specs=pl.BlockSpec((tm, D_MODEL), row),
        out_shape=jax.ShapeDtypeStruct((T, D_MODEL), jnp.float32),
        compiler_params=pltpu.CompilerParams(
            dimension_semantics=("arbitrary",),
            vmem_limit_bytes=VMEM_LIMIT_BYTES),
        name="tail",
    )(x2, p2, attn, sga, conv, norm_attn, w_out, norm_ple, w_gate, b_gate, w_ple, norm_final)


def _rope_tables(seq):
    half = ROT_DIM // 2
    inv = jnp.power(jnp.float32(ROPE_THETA),
                    -jnp.arange(0, ROT_DIM, 2, dtype=jnp.float32) / ROT_DIM)
    ang = jnp.arange(seq, dtype=jnp.float32)[:, None] * inv[None, :]
    cos, sin = jnp.cos(ang), jnp.sin(ang)
    pad = HEAD_DIM - ROT_DIM
    ones = jnp.ones((seq, pad), jnp.float32)
    zeros = jnp.zeros((seq, pad), jnp.float32)
    zh = jnp.zeros((seq, half), jnp.float32)
    per_head = lambda parts: jnp.tile(jnp.concatenate(parts, axis=1), (1, LANES // HEAD_DIM))
    return (per_head([cos, cos, ones]),
            per_head([-sin, zh, zeros]),
            per_head([zh, sin, zeros]))


def _trunk(x, p, tables, norm_mix, w_in, conv_w, norm_attn, norm_conv, w_out,
           norm_ple, w_gate, b_gate, w_ple, norm_final):
    B, S, _ = x.shape
    assert S % (16 * Q_BLOCK) == 0
    x2 = x.reshape(B * S, D_MODEL)
    p2 = p.reshape(B * S, PLE_DIM)
    cos_t, sinlo_t, sinhi_t = tables
    q, k, v, sga, conv = _in_proj(x2, norm_mix, w_in, cos_t[:S], sinlo_t[:S], sinhi_t[:S],
                                  conv_w, norm_conv, S, tm=1024)
    attn = _dil_attn(q, k, v, B, S)
    y = _tail(x2, p2, attn, sga, conv, norm_attn, w_out, norm_ple, w_gate, b_gate, w_ple,
              norm_final, tm=1024)
    return y.reshape(B, S, D_MODEL)


def kernel(x_prompt, x_sample, p_prompt, p_sample, norm_mix, w_in, conv_w,
           norm_attn_out, norm_conv_out, w_out, norm_ple, w_ple_gate, b_ple_gate,
           w_ple_proj, norm_final):
    depth = w_in.shape[0]
    assert depth == 1, "single-layer trunk"
    bf16 = jnp.bfloat16
    tables = _rope_tables(max(x_prompt.shape[1], x_sample.shape[1]))
    params = (norm_mix[0][None], w_in[0].astype(bf16), conv_w[0],
              norm_attn_out[0][None], norm_conv_out[0][None], w_out[0].astype(bf16),
              norm_ple[0][None], w_ple_gate[0].astype(bf16), b_ple_gate[0][None],
              w_ple_proj[0].astype(bf16), norm_final[None])
    y_prompt = _trunk(x_prompt, p_prompt[0], tables, *params)
    y_sample = _trunk(x_sample, p_sample[0], tables, *params)
    return (y_prompt, y_sample)
```

```python
import functools

import jax
import jax.numpy as jnp
from jax import lax
from jax.experimental import pallas as pl
from jax.experimental.pallas import tpu as pltpu

D_MODEL = 1024
ATTN_WIDTH = 512
CONV_WIDTH = 512
HEAD_DIM = 64
ROT_DIM = 16
ROPE_THETA = 500000.0
HALF_BAND = 64
PLE_DIM = 256
EPS = 1e-6
NEG = -1e30
LOG2_E = 1.4426950408889634

LANES = 128
F32_SUBLANES = 8
BF16_SUBLANES = 16
Q_BLOCK = 128
Q_BLOCKS_PER_STEP = 4
COPY_ROWS = 128
MERGE_UNROLL = 8
VMEM_LIMIT_BYTES = 56 * 1024 * 1024

_TRANS_B = (((1,), (1,)), ((), ()))


def _rms_scale(v):
    return lax.rsqrt(jnp.mean(v * v, axis=-1, keepdims=True) + EPS)


def _in_proj_kernel(x_ref, xprev_ref, xnext_ref, g_ref, w_ref, cos_ref, sinlo_ref, sinhi_ref,
                    convw_ref, nconv_ref, q_ref, k_ref, v_ref, sga_ref, conv_ref,
                    *, tiles_per_seq):
    tm = x_ref.shape[0]

    def normed(x):
        return (x * _rms_scale(x) * g_ref[...]).astype(jnp.bfloat16)

    h = normed(x_ref[...])

    def cols(j, hh=None, n=1):
        return jnp.dot(h if hh is None else hh,
                       w_ref[:, j * ATTN_WIDTH:(j + n) * ATTN_WIDTH],
                       preferred_element_type=jnp.float32)

    cos = cos_ref[...]
    sinlo = sinlo_ref[...]
    sinhi = sinhi_ref[...]

    def rope(t):
        outs = []
        for c in range(ATTN_WIDTH // LANES):
            tc = t[:, c * LANES:(c + 1) * LANES]
            up = pltpu.roll(tc, LANES - ROT_DIM // 2, axis=1)
            dn = pltpu.roll(tc, ROT_DIM // 2, axis=1)
            outs.append(tc * cos + up * sinlo + dn * sinhi)
        return jnp.concatenate(outs, axis=1)

    first_head = lax.broadcasted_iota(jnp.int32, (1, LANES), 1) < HEAD_DIM

    def per_head(t, fill):
        outs = []
        for c in range(ATTN_WIDTH // LANES):
            tc = t[:, c * LANES:(c + 1) * LANES]
            outs += [jnp.where(first_head, tc, fill), jnp.where(first_head, fill, tc)]
        return jnp.concatenate(outs, axis=1).astype(jnp.bfloat16)

    def words(t):
        return pltpu.bitcast(t, jnp.uint32)

    halo = normed(jnp.concatenate([xprev_ref[...], xnext_ref[...]], axis=0))
    cc_ch = cols(5, jnp.concatenate([h, halo], axis=0), 2)
    u_ext = cc_ch[:, :CONV_WIDTH] * cc_ch[:, CONV_WIDTH:]
    u = u_ext[:tm]
    t_in_seq = pl.program_id(0) % tiles_per_seq
    prev_row = u_ext[tm + F32_SUBLANES - 1:tm + F32_SUBLANES, :]
    next_row = u_ext[tm + F32_SUBLANES:tm + F32_SUBLANES + 1, :]
    prev_row = jnp.where(t_in_seq == 0, 0.0, prev_row)
    next_row = jnp.where(t_in_seq == tiles_per_seq - 1, 0.0, next_row)

    q_ref[...] = words(per_head(rope(cols(0)) * (HEAD_DIM ** -0.5 * LOG2_E), 0.0))

    ridx = lax.broadcasted_iota(jnp.int32, (tm, 1), 0)
    u_before = jnp.where(ridx == 0, prev_row, pltpu.roll(u, 1, axis=0))
    u_after = jnp.where(ridx == tm - 1, next_row, pltpu.roll(u, tm - 1, axis=0))
    cw = convw_ref[...]
    taps = cw[0:1] * u_before + cw[1:2] * u + cw[2:3] * u_after

    k_ref[...] = words(rope(cols(1)).astype(jnp.bfloat16))

    conv = cols(4) * taps
    conv = conv * _rms_scale(conv) * nconv_ref[...]

    v_ref[...] = words(per_head(cols(2), 1.0))

    conv_ref[...] = (conv * jax.nn.silu(cols(7))).astype(jnp.bfloat16)
    sga_ref[...] = jax.nn.silu(cols(3)).astype(jnp.bfloat16)


def _in_proj(x2, g, w_bf16, cos_t, sinlo_t, sinhi_t, conv_w, norm_conv, seq, tm):
    T = x2.shape[0]
    nseq = seq // tm
    halo_blocks = T // F32_SUBLANES
    per_tile = tm // F32_SUBLANES
    row = lambda i: (i, 0)
    fixed = lambda i: (0, 0)
    pos = lambda i: (i % nseq, 0)
    prev = lambda i: (jnp.maximum(i * per_tile - 1, 0), 0)
    nxt = lambda i: (jnp.minimum((i + 1) * per_tile, halo_blocks - 1), 0)
    out = jax.ShapeDtypeStruct((T, ATTN_WIDTH), jnp.bfloat16)
    wout = jax.ShapeDtypeStruct((T // 2, ATTN_WIDTH), jnp.uint32)
    wout2 = jax.ShapeDtypeStruct((T // 2, 2 * ATTN_WIDTH), jnp.uint32)
    spec = pl.BlockSpec((tm, ATTN_WIDTH), row)
    wspec = pl.BlockSpec((tm // 2, ATTN_WIDTH), row)
    wspec2 = pl.BlockSpec((tm // 2, 2 * ATTN_WIDTH), row)
    full = lambda a: pl.BlockSpec(a.shape, fixed)
    return pl.pallas_call(
        functools.partial(_in_proj_kernel, tiles_per_seq=nseq),
        grid=(T // tm,),
        in_specs=[
            pl.BlockSpec((tm, D_MODEL), row),
            pl.BlockSpec((F32_SUBLANES, D_MODEL), prev),
            pl.BlockSpec((F32_SUBLANES, D_MODEL), nxt),
            full(g), full(w_bf16),
            pl.BlockSpec((tm, LANES), pos),
            pl.BlockSpec((tm, LANES), pos),
            pl.BlockSpec((tm, LANES), pos),
            full(conv_w), full(norm_conv),
        ],
        out_specs=[wspec2, wspec, wspec2, spec, spec],
        out_shape=[wout2, wout, wout2, out, out],
        compiler_params=pltpu.CompilerParams(
            dimension_semantics=("arbitrary",),
            vmem_limit_bytes=VMEM_LIMIT_BYTES),
        name="in_proj",
    )(x2, x2, x2, g, w_bf16, cos_t, sinlo_t, sinhi_t, conv_w, norm_conv)


def _dil_attn_kernel(qa_ref, qb_ref, k_ref, va_ref, vb_ref, out_ref, fbuf, a16, b16, bias_scr,
                     s_scr, p_scr, m_scr, *, seq):
    f32, bf16, u32 = jnp.float32, jnp.bfloat16, jnp.uint32
    len4, len16 = seq // 4, seq // 16
    first_head = lax.broadcasted_iota(jnp.int32, (1, LANES), 1) < HEAD_DIM
    inputs = (qa_ref, qb_ref, k_ref, va_ref, vb_ref)
    n_copy = seq // (2 * COPY_ROWS)

    def rows_bf16(ref, lead, start, size):
        words = pl.ds(pl.multiple_of(start // 2, BF16_SUBLANES // 2), size // 2)
        return pltpu.bitcast(ref[(*lead, words, slice(None))], bf16)

    def split_row_pairs(words, lead, word0, dst, x, row_even, row_odd):
        w = words[(*lead, pl.ds(word0, COPY_ROWS, stride=2), slice(None))]
        even, odd = (pltpu.unpack_elementwise(w, index=i, packed_dtype=bf16,
                                              unpacked_dtype=f32).astype(bf16) for i in (0, 1))
        half = COPY_ROWS // 2
        dst[x, pl.ds(pl.multiple_of(row_even // 2, half), half), :] = pltpu.bitcast(even, u32)
        dst[x, pl.ds(pl.multiple_of(row_odd // 2, half), half), :] = pltpu.bitcast(odd, u32)

    def split4(c):
        chunks = len4 // COPY_ROWS
        pair = c // chunks
        u0 = (c % chunks) * COPY_ROWS
        for x in range(5):
            split_row_pairs(inputs[x], (), pair + 2 * u0, a16, x,
                            2 * pair * len4 + u0, (2 * pair + 1) * len4 + u0)

    def split16(c):
        chunks = len16 // COPY_ROWS
        w0 = (c % chunks) * COPY_ROWS
        pair = (c // chunks) % 2
        r4 = c // (2 * chunks)
        for x in range(5):
            split_row_pairs(a16, (x,), r4 * (len4 // 2) + pair + 2 * w0,
                            b16, x, (r4 + 8 * pair) * len16 + w0,
                            (r4 + 8 * pair + 4) * len16 + w0)

    win_max = Q_BLOCK + 2 * HALF_BAND
    rel = (lax.broadcasted_iota(jnp.int32, (Q_BLOCK, win_max), 0)
           - lax.broadcasted_iota(jnp.int32, (Q_BLOCK, win_max), 1))
    for n in range(3):
        bias_scr[n] = jnp.where(jnp.abs(rel + n * HALF_BAND) <= HALF_BAND, 0.0, NEG)

    def run_group(src, dil, out_slot, relayout=None):
        cls_len = seq // dil
        win = min(cls_len, win_max)
        blocks_per_class = cls_len // Q_BLOCK

        n_steps = seq // (Q_BLOCK * Q_BLOCKS_PER_STEP)

        def blocks(step):
            for jj in range(Q_BLOCKS_PER_STEP):
                t = step * Q_BLOCKS_PER_STEP + jj
                row = pl.multiple_of(t * Q_BLOCK, Q_BLOCK)
                cls = t // blocks_per_class
                q0 = (t % blocks_per_class) * Q_BLOCK
                w0 = jnp.clip(q0 - HALF_BAND, 0, cls_len - win)
                start = pl.multiple_of(cls * cls_len + w0, BF16_SUBLANES)
                yield jj, row, cls, q0, w0, start

        def scores(step, par):
            for jj, row, cls, q0, w0, start in blocks(step):
                bias = bias_scr[(q0 - w0) // HALF_BAND, :, :win]
                q_ab = jnp.concatenate([src(0, row, Q_BLOCK), src(1, row, Q_BLOCK)], axis=0)
                s = lax.dot_general(q_ab, src(2, start, win), _TRANS_B,
                                    preferred_element_type=f32)
                s_scr[par, 2 * jj, :, :win] = s[:Q_BLOCK] + bias
                s_scr[par, 2 * jj + 1, :, :win] = s[Q_BLOCK:] + bias

        def softmax(par):
            for jj in range(Q_BLOCKS_PER_STEP):
                maxes = []
                for idx in (2 * jj, 2 * jj + 1):
                    s = s_scr[par, idx, :, :win]
                    m = jnp.max(s, axis=-1, keepdims=True)
                    p_scr[par, idx, :, :win] = jnp.exp2((s - m).astype(bf16))
                    maxes.append(m)
                m_scr[par, jj] = jnp.where(first_head, maxes[0], maxes[1])

        def values(step, par):
            for jj, row, cls, q0, w0, start in blocks(step):
                r_a = jnp.dot(p_scr[par, 2 * jj, :, :win], src(3, start, win),
                              preferred_element_type=f32)
                r_b = jnp.dot(p_scr[par, 2 * jj + 1, :, :win], src(4, start, win),
                              preferred_element_type=f32)
                o_un = jnp.where(first_head, r_a, r_b)
                den = pltpu.roll(jnp.where(first_head, r_b, r_a), HEAD_DIM, axis=1)
                toks = pl.ds(cls + dil * q0, Q_BLOCK, stride=dil) if dil > 1 else pl.ds(row, Q_BLOCK)
                fbuf[out_slot, toks, :] = o_un / den
                fbuf[out_slot + 1, toks, :] = m_scr[par, jj] + jnp.log2(den)

        assert n_steps % 2 == 0 and n_steps >= 2
        n_loop = (n_steps - 2) // 2
        per_part = n_copy // (n_loop + 2) if relayout else 0

        def fill(first, count):
            for i in range(count):
                relayout(first + i)

        scores(0, 0)
        softmax(0)
        scores(1, 1)
        fill(0, per_part)

        def body(j, carry):
            step = 2 * j
            values(step, 0)
            softmax(1)
            scores(step + 2, 0)
            fill((j + 1) * per_part, per_part)
            values(step + 1, 1)
            softmax(0)
            scores(step + 3, 1)
            return carry

        lax.fori_loop(0, n_loop, body, 0)
        values(n_steps - 2, 0)
        softmax(1)
        if relayout:
            fill((n_loop + 1) * per_part, n_copy - (n_loop + 1) * per_part)
        values(n_steps - 1, 1)

    run_group(lambda n, start, size: rows_bf16(inputs[n], (), start, size), 1, 4, split4)
    run_group(lambda n, start, size: rows_bf16(a16, (n,), start, size), 4, 2, split16)
    run_group(lambda n, start, size: rows_bf16(b16, (n,), start, size), 16, 0)

    def merge(c, carry):
        rows = pl.ds(pl.multiple_of(c * COPY_ROWS, COPY_ROWS), COPY_ROWS)
        lse = [fbuf[2 * g + 1, rows, :] for g in range(3)]
        top = jnp.maximum(jnp.maximum(lse[0], lse[1]), lse[2])
        w = [jnp.exp2(l - top) for l in lse]
        num = w[0] * fbuf[0, rows, :] + w[1] * fbuf[2, rows, :] + w[2] * fbuf[4, rows, :]
        out_ref[rows, :] = (num / (w[0] + w[1] + w[2])).astype(out_ref.dtype)
        return carry

    lax.fori_loop(0, seq // COPY_ROWS, merge, 0, unroll=MERGE_UNROLL)


def _dil_attn(q, k, v, batch, seq):
    words = seq // 2
    pair = pl.BlockSpec((words, LANES), lambda b, c: (b, c))
    head_a = pl.BlockSpec((words, LANES), lambda b, c: (b, 2 * c))
    head_b = pl.BlockSpec((words, LANES), lambda b, c: (b, 2 * c + 1))
    n_tiles = 2 * Q_BLOCKS_PER_STEP
    win = Q_BLOCK + 2 * HALF_BAND
    return pl.pallas_call(
        functools.partial(_dil_attn_kernel, seq=seq),
        grid=(batch, ATTN_WIDTH // LANES),
        in_specs=[head_a, head_b, pair, head_a, head_b],
        out_specs=pl.BlockSpec((seq, LANES), lambda b, c: (b, c)),
        out_shape=jax.ShapeDtypeStruct((batch * seq, ATTN_WIDTH), jnp.bfloat16),
        scratch_shapes=[
            pltpu.VMEM((6, seq, LANES), jnp.float32),
            pltpu.VMEM((5, words, LANES), jnp.uint32),
            pltpu.VMEM((5, words, LANES), jnp.uint32),
            pltpu.VMEM((3, Q_BLOCK, win), jnp.float32),
            pltpu.VMEM((2, n_tiles, Q_BLOCK, win), jnp.float32),
            pltpu.VMEM((2, n_tiles, Q_BLOCK, win), jnp.bfloat16),
            pltpu.VMEM((2, Q_BLOCKS_PER_STEP, Q_BLOCK, LANES), jnp.float32),
        ],
        compiler_params=pltpu.CompilerParams(
            dimension_semantics=("arbitrary", "arbitrary"),
            vmem_limit_bytes=VMEM_LIMIT_BYTES),
        name="dil_attn",
    )(q, q, k, v, v)


def _tail_kernel(x_ref, p_ref, attn_ref, sga_ref, conv_ref, na_ref, wout_ref, nple_ref,
                 wgate_ref, bgate_ref, wple_ref, nfin_ref, y_ref):
    f32 = jnp.float32
    attn = attn_ref[...].astype(f32)
    gated_attn = attn * _rms_scale(attn) * na_ref[...] * sga_ref[...].astype(f32)
    merged = jnp.concatenate([gated_attn.astype(jnp.bfloat16), conv_ref[...]], axis=1)
    x1 = x_ref[...] + jnp.dot(merged, wout_ref[...], preferred_element_type=f32)

    r = (x1 * _rms_scale(x1) * nple_ref[...]).astype(jnp.bfloat16)
    gate = jax.nn.sigmoid(jnp.dot(r, wgate_ref[...], preferred_element_type=f32)
                          + bgate_ref[...])
    emb = jnp.dot(p_ref[...].astype(jnp.bfloat16), wple_ref[...],
                  preferred_element_type=f32)
    x2 = x1 + emb * gate
    y_ref[...] = x2 * _rms_scale(x2) * nfin_ref[...]


def _tail(x2, p2, attn, sga, conv, norm_attn, w_out, norm_ple, w_gate, b_gate, w_ple,
          norm_final, tm):
    T = x2.shape[0]
    row = lambda i: (i, 0)
    fixed = lambda i: (0, 0)
    a_spec = pl.BlockSpec((tm, ATTN_WIDTH), row)
    full = lambda a: pl.BlockSpec(a.shape, fixed)
    return pl.pallas_call(
        _tail_kernel,
        grid=(T // tm,),
        in_specs=[
            pl.BlockSpec((tm, D_MODEL), row),
            pl.BlockSpec((tm, PLE_DIM), row),
            a_spec, a_spec, a_spec,
            full(norm_attn), full(w_out), full(norm_ple), full(w_gate), full(b_gate),
            full(w_ple), full(norm_final),
        ],
        out_specs=pl.BlockSpec((tm, D_MODEL), row),
        out_shape=jax.ShapeDtypeStruct((T, D_MODEL), jnp.float32),
        compiler_params=pltpu.CompilerParams(
            dimension_semantics=("arbitrary",),
            vmem_limit_bytes=VMEM_LIMIT_BYTES),
        name="tail",
    )(x2, p2, attn, sga, conv, norm_attn, w_out, norm_ple, w_gate, b_gate, w_ple, norm_final)


def _rope_tables(seq):
    half = ROT_DIM // 2
    inv = jnp.power(jnp.float32(ROPE_THETA),
                    -jnp.arange(0, ROT_DIM, 2, dtype=jnp.float32) / ROT_DIM)
    ang = jnp.arange(seq, dtype=jnp.float32)[:, None] * inv[None, :]
    cos, sin = jnp.cos(ang), jnp.sin(ang)
    pad = HEAD_DIM - ROT_DIM
    ones = jnp.ones((seq, pad), jnp.float32)
    zeros = jnp.zeros((seq, pad), jnp.float32)
    zh = jnp.zeros((seq, half), jnp.float32)
    per_head = lambda parts: jnp.tile(jnp.concatenate(parts, axis=1), (1, LANES // HEAD_DIM))
    return (per_head([cos, cos, ones]),
            per_head([-sin, zh, zeros]),
            per_head([zh, sin, zeros]))


def _trunk(x, p, tables, norm_mix, w_in, conv_w, norm_attn, norm_conv, w_out,
           norm_ple, w_gate, b_gate, w_ple, norm_final):
    B, S, _ = x.shape
    assert S % (16 * Q_BLOCK) == 0
    x2 = x.reshape(B * S, D_MODEL)
    p2 = p.reshape(B * S, PLE_DIM)
    cos_t, sinlo_t, sinhi_t = tables
    q, k, v, sga, conv = _in_proj(x2, norm_mix, w_in, cos_t[:S], sinlo_t[:S], sinhi_t[:S],
                                  conv_w, norm_conv, S, tm=1024)
    attn = _dil_attn(q, k, v, B, S)
    y = _tail(x2, p2, attn, sga, conv, norm_attn, w_out, norm_ple, w_gate, b_gate, w_ple,
              norm_final, tm=1024)
    return y.reshape(B, S, D_MODEL)


def kernel(x_prompt, x_sample, p_prompt, p_sample, norm_mix, w_in, conv_w,
           norm_attn_out, norm_conv_out, w_out, norm_ple, w_ple_gate, b_ple_gate,
           w_ple_proj, norm_final):
    depth = w_in.shape[0]
    assert depth == 1, "single-layer trunk"
    bf16 = jnp.bfloat16
    tables = _rope_tables(max(x_prompt.shape[1], x_sample.shape[1]))
    params = (norm_mix[0][None], w_in[0].astype(bf16), conv_w[0],
              norm_attn_out[0][None], norm_conv_out[0][None], w_out[0].astype(bf16),
              norm_ple[0][None], w_ple_gate[0].astype(bf16), b_ple_gate[0][None],
              w_ple_proj[0].astype(bf16), norm_final[None])
    y_prompt = _trunk(x_prompt, p_prompt[0], tables, *params)
    y_sample = _trunk(x_sample, p_sample[0], tables, *params)
    return (y_prompt, y_sample)
```

```python
import functools

import jax
import jax.numpy as jnp
from jax import lax
from jax.experimental import pallas as pl
from jax.experimental.pallas import tpu as pltpu

D_MODEL = 1024
ATTN_WIDTH = 512
CONV_WIDTH = 512
HEAD_DIM = 64
ROT_DIM = 16
ROPE_THETA = 500000.0
HALF_BAND = 64
PLE_DIM = 256
EPS = 1e-6
NEG = -1e30
LOG2_E = 1.4426950408889634

LANES = 128
F32_SUBLANES = 8
BF16_SUBLANES = 16
Q_BLOCK = 128
Q_BLOCKS_PER_STEP = 4
COPY_ROWS = 128
MERGE_UNROLL = 8
VMEM_LIMIT_BYTES = 56 * 1024 * 1024

_TRANS_B = (((1,), (1,)), ((), ()))


def _rms_scale(v):
    return lax.rsqrt(jnp.mean(v * v, axis=-1, keepdims=True) + EPS)


def _in_proj_kernel(x_ref, xprev_ref, xnext_ref, g_ref, w_ref, cos_ref, sinlo_ref, sinhi_ref,
                    convw_ref, nconv_ref, q_ref, k_ref, v_ref, sga_ref, conv_ref,
                    *, tiles_per_seq):
    tm = x_ref.shape[0]

    def normed(x):
        return (x * _rms_scale(x) * g_ref[...]).astype(jnp.bfloat16)

    h = normed(x_ref[...])

    def cols(j, hh=None, n=1):
        return jnp.dot(h if hh is None else hh,
                       w_ref[:, j * ATTN_WIDTH:(j + n) * ATTN_WIDTH],
                       preferred_element_type=jnp.float32)

    cos = cos_ref[...]
    sinlo = sinlo_ref[...]
    sinhi = sinhi_ref[...]

    def rope(t):
        outs = []
        for c in range(ATTN_WIDTH // LANES):
            tc = t[:, c * LANES:(c + 1) * LANES]
            up = pltpu.roll(tc, LANES - ROT_DIM // 2, axis=1)
            dn = pltpu.roll(tc, ROT_DIM // 2, axis=1)
            outs.append(tc * cos + up * sinlo + dn * sinhi)
        return jnp.concatenate(outs, axis=1)

    first_head = lax.broadcasted_iota(jnp.int32, (1, LANES), 1) < HEAD_DIM

    def pairs(t):
        return [t[:, c * LANES:(c + 1) * LANES] for c in range(ATTN_WIDTH // LANES)]

    def per_head(t, fill):
        outs = []
        for tc in pairs(t):
            outs += [jnp.where(first_head, tc, fill), jnp.where(first_head, fill, tc)]
        return outs

    def put(ref, blocks):
        for n, blk in enumerate(blocks):
            ref[n] = pltpu.bitcast(blk.astype(jnp.bfloat16), jnp.uint32)

    halo = normed(jnp.concatenate([xprev_ref[...], xnext_ref[...]], axis=0))
    cc_ch = cols(5, jnp.concatenate([h, halo], axis=0), 2)
    u_ext = cc_ch[:, :CONV_WIDTH] * cc_ch[:, CONV_WIDTH:]
    u = u_ext[:tm]
    t_in_seq = pl.program_id(0) % tiles_per_seq
    prev_row = u_ext[tm + F32_SUBLANES - 1:tm + F32_SUBLANES, :]
    next_row = u_ext[tm + F32_SUBLANES:tm + F32_SUBLANES + 1, :]
    prev_row = jnp.where(t_in_seq == 0, 0.0, prev_row)
    next_row = jnp.where(t_in_seq == tiles_per_seq - 1, 0.0, next_row)

    put(q_ref, per_head(rope(cols(0)) * (HEAD_DIM ** -0.5 * LOG2_E), 0.0))

    ridx = lax.broadcasted_iota(jnp.int32, (tm, 1), 0)
    u_before = jnp.where(ridx == 0, prev_row, pltpu.roll(u, 1, axis=0))
    u_after = jnp.where(ridx == tm - 1, next_row, pltpu.roll(u, tm - 1, axis=0))
    cw = convw_ref[...]
    taps = cw[0:1] * u_before + cw[1:2] * u + cw[2:3] * u_after

    put(k_ref, pairs(rope(cols(1))))

    conv = cols(4) * taps
    conv = conv * _rms_scale(conv) * nconv_ref[...]

    put(v_ref, per_head(cols(2), 1.0))

    conv_ref[...] = (conv * jax.nn.silu(cols(7))).astype(jnp.bfloat16)
    sga_ref[...] = jax.nn.silu(cols(3)).astype(jnp.bfloat16)


def _in_proj(x2, g, w_bf16, cos_t, sinlo_t, sinhi_t, conv_w, norm_conv, seq, tm):
    T = x2.shape[0]
    nseq = seq // tm
    halo_blocks = T // F32_SUBLANES
    per_tile = tm // F32_SUBLANES
    row = lambda i: (i, 0)
    fixed = lambda i: (0, 0)
    pos = lambda i: (i % nseq, 0)
    prev = lambda i: (jnp.maximum(i * per_tile - 1, 0), 0)
    nxt = lambda i: (jnp.minimum((i + 1) * per_tile, halo_blocks - 1), 0)
    out = jax.ShapeDtypeStruct((T, ATTN_WIDTH), jnp.bfloat16)
    n_pairs = ATTN_WIDTH // LANES
    wout = jax.ShapeDtypeStruct((n_pairs, T // 2, LANES), jnp.uint32)
    wout2 = jax.ShapeDtypeStruct((2 * n_pairs, T // 2, LANES), jnp.uint32)
    spec = pl.BlockSpec((tm, ATTN_WIDTH), row)
    slabs = lambda i: (0, i, 0)
    wspec = pl.BlockSpec((n_pairs, tm // 2, LANES), slabs)
    wspec2 = pl.BlockSpec((2 * n_pairs, tm // 2, LANES), slabs)
    full = lambda a: pl.BlockSpec(a.shape, fixed)
    return pl.pallas_call(
        functools.partial(_in_proj_kernel, tiles_per_seq=nseq),
        grid=(T // tm,),
        in_specs=[
            pl.BlockSpec((tm, D_MODEL), row),
            pl.BlockSpec((F32_SUBLANES, D_MODEL), prev),
            pl.BlockSpec((F32_SUBLANES, D_MODEL), nxt),
            full(g), full(w_bf16),
            pl.BlockSpec((tm, LANES), pos),
            pl.BlockSpec((tm, LANES), pos),
            pl.BlockSpec((tm, LANES), pos),
            full(conv_w), full(norm_conv),
        ],
        out_specs=[wspec2, wspec, wspec2, spec, spec],
        out_shape=[wout2, wout, wout2, out, out],
        compiler_params=pltpu.CompilerParams(
            dimension_semantics=("arbitrary",),
            vmem_limit_bytes=VMEM_LIMIT_BYTES),
        name="in_proj",
    )(x2, x2, x2, g, w_bf16, cos_t, sinlo_t, sinhi_t, conv_w, norm_conv)


def _dil_attn_kernel(qa_ref, qb_ref, k_ref, va_ref, vb_ref, out_ref, fbuf, a16, b16, bias_scr,
                     s_scr, p_scr, m_scr, *, seq):
    f32, bf16, u32 = jnp.float32, jnp.bfloat16, jnp.uint32
    len4, len16 = seq // 4, seq // 16
    first_head = lax.broadcasted_iota(jnp.int32, (1, LANES), 1) < HEAD_DIM
    inputs = (qa_ref, qb_ref, k_ref, va_ref, vb_ref)
    n_copy = seq // (2 * COPY_ROWS)

    def rows_bf16(ref, lead, start, size):
        words = pl.ds(pl.multiple_of(start // 2, BF16_SUBLANES // 2), size // 2)
        return pltpu.bitcast(ref[(*lead, words, slice(None))], bf16)

    def split_row_pairs(words, lead, word0, dst, x, row_even, row_odd):
        w = words[(*lead, pl.ds(word0, COPY_ROWS, stride=2), slice(None))]
        even, odd = (pltpu.unpack_elementwise(w, index=i, packed_dtype=bf16,
                                              unpacked_dtype=f32).astype(bf16) for i in (0, 1))
        half = COPY_ROWS // 2
        dst[x, pl.ds(pl.multiple_of(row_even // 2, half), half), :] = pltpu.bitcast(even, u32)
        dst[x, pl.ds(pl.multiple_of(row_odd // 2, half), half), :] = pltpu.bitcast(odd, u32)

    def split4(c):
        chunks = len4 // COPY_ROWS
        pair = c // chunks
        u0 = (c % chunks) * COPY_ROWS
        for x in range(5):
            split_row_pairs(inputs[x], (), pair + 2 * u0, a16, x,
                            2 * pair * len4 + u0, (2 * pair + 1) * len4 + u0)

    def split16(c):
        chunks = len16 // COPY_ROWS
        w0 = (c % chunks) * COPY_ROWS
        pair = (c // chunks) % 2
        r4 = c // (2 * chunks)
        for x in range(5):
            split_row_pairs(a16, (x,), r4 * (len4 // 2) + pair + 2 * w0,
                            b16, x, (r4 + 8 * pair) * len16 + w0,
                            (r4 + 8 * pair + 4) * len16 + w0)

    win_max = Q_BLOCK + 2 * HALF_BAND
    rel = (lax.broadcasted_iota(jnp.int32, (Q_BLOCK, win_max), 0)
           - lax.broadcasted_iota(jnp.int32, (Q_BLOCK, win_max), 1))
    for n in range(3):
        bias_scr[n] = jnp.where(jnp.abs(rel + n * HALF_BAND) <= HALF_BAND, 0.0, NEG)

    def run_group(src, dil, out_slot, relayout=None):
        cls_len = seq // dil
        win = min(cls_len, win_max)
        blocks_per_class = cls_len // Q_BLOCK

        n_steps = seq // (Q_BLOCK * Q_BLOCKS_PER_STEP)

        def blocks(step):
            for jj in range(Q_BLOCKS_PER_STEP):
                t = step * Q_BLOCKS_PER_STEP + jj
                row = pl.multiple_of(t * Q_BLOCK, Q_BLOCK)
                cls = t // blocks_per_class
                q0 = (t % blocks_per_class) * Q_BLOCK
                w0 = jnp.clip(q0 - HALF_BAND, 0, cls_len - win)
                start = pl.multiple_of(cls * cls_len + w0, BF16_SUBLANES)
                yield jj, row, cls, q0, w0, start

        def scores(step, par):
            for jj, row, cls, q0, w0, start in blocks(step):
                bias = bias_scr[(q0 - w0) // HALF_BAND, :, :win]
                q_ab = jnp.concatenate([src(0, row, Q_BLOCK), src(1, row, Q_BLOCK)], axis=0)
                s = lax.dot_general(q_ab, src(2, start, win), _TRANS_B,
                                    preferred_element_type=f32)
                s_scr[par, 2 * jj, :, :win] = s[:Q_BLOCK] + bias
                s_scr[par, 2 * jj + 1, :, :win] = s[Q_BLOCK:] + bias

        def softmax(par):
            for jj in range(Q_BLOCKS_PER_STEP):
                maxes = []
                for idx in (2 * jj, 2 * jj + 1):
                    s = s_scr[par, idx, :, :win]
                    m = jnp.max(s, axis=-1, keepdims=True)
                    p_scr[par, idx, :, :win] = jnp.exp2(s - m).astype(bf16)
                    maxes.append(m)
                m_scr[par, jj] = jnp.where(first_head, maxes[0], maxes[1])

        def values(step, par):
            for jj, row, cls, q0, w0, start in blocks(step):
                r_a = jnp.dot(p_scr[par, 2 * jj, :, :win], src(3, start, win),
                              preferred_element_type=f32)
                r_b = jnp.dot(p_scr[par, 2 * jj + 1, :, :win], src(4, start, win),
                              preferred_element_type=f32)
                o_un = jnp.where(first_head, r_a, r_b)
                den = pltpu.roll(jnp.where(first_head, r_b, r_a), HEAD_DIM, axis=1)
                toks = pl.ds(cls + dil * q0, Q_BLOCK, stride=dil) if dil > 1 else pl.ds(row, Q_BLOCK)
                fbuf[out_slot, toks, :] = o_un / den
                fbuf[out_slot + 1, toks, :] = m_scr[par, jj] + jnp.log2(den)

        assert n_steps % 2 == 0 and n_steps >= 2
        n_loop = (n_steps - 2) // 2
        per_part = n_copy // (n_loop + 2) if relayout else 0

        def fill(first, count):
            for i in range(count):
                relayout(first + i)

        scores(0, 0)
        softmax(0)
        scores(1, 1)
        fill(0, per_part)

        def body(j, carry):
            step = 2 * j
            values(step, 0)
            softmax(1)
            scores(step + 2, 0)
            fill((j + 1) * per_part, per_part)
            values(step + 1, 1)
            softmax(0)
            scores(step + 3, 1)
            return carry

        lax.fori_loop(0, n_loop, body, 0)
        values(n_steps - 2, 0)
        softmax(1)
        if relayout:
            fill((n_loop + 1) * per_part, n_copy - (n_loop + 1) * per_part)
        values(n_steps - 1, 1)

    run_group(lambda n, start, size: rows_bf16(inputs[n], (), start, size), 1, 4, split4)
    run_group(lambda n, start, size: rows_bf16(a16, (n,), start, size), 4, 2, split16)
    run_group(lambda n, start, size: rows_bf16(b16, (n,), start, size), 16, 0)

    def merge(c, carry):
        rows = pl.ds(pl.multiple_of(c * COPY_ROWS, COPY_ROWS), COPY_ROWS)
        lse = [fbuf[2 * g + 1, rows, :] for g in range(3)]
        top = jnp.maximum(jnp.maximum(lse[0], lse[1]), lse[2])
        w = [jnp.exp2(l - top) for l in lse]
        num = w[0] * fbuf[0, rows, :] + w[1] * fbuf[2, rows, :] + w[2] * fbuf[4, rows, :]
        out_ref[rows, :] = (num / (w[0] + w[1] + w[2])).astype(out_ref.dtype)
        return carry

    lax.fori_loop(0, seq // COPY_ROWS, merge, 0, unroll=MERGE_UNROLL)


def _dil_attn(q, k, v, batch, seq):
    words = seq // 2
    pair = pl.BlockSpec((None, words, LANES), lambda b, c: (c, b, 0))
    head_a = pl.BlockSpec((None, words, LANES), lambda b, c: (2 * c, b, 0))
    head_b = pl.BlockSpec((None, words, LANES), lambda b, c: (2 * c + 1, b, 0))
    n_tiles = 2 * Q_BLOCKS_PER_STEP
    win = Q_BLOCK + 2 * HALF_BAND
    return pl.pallas_call(
        functools.partial(_dil_attn_kernel, seq=seq),
        grid=(batch, ATTN_WIDTH // LANES),
        in_specs=[head_a, head_b, pair, head_a, head_b],
        out_specs=pl.BlockSpec((seq, LANES), lambda b, c: (b, c)),
        out_shape=jax.ShapeDtypeStruct((batch * seq, ATTN_WIDTH), jnp.bfloat16),
        scratch_shapes=[
            pltpu.VMEM((6, seq, LANES), jnp.float32),
            pltpu.VMEM((5, words, LANES), jnp.uint32),
            pltpu.VMEM((5, words, LANES), jnp.uint32),
            pltpu.VMEM((3, Q_BLOCK, win), jnp.float32),
            pltpu.VMEM((2, n_tiles, Q_BLOCK, win), jnp.float32),
            pltpu.VMEM((2, n_tiles, Q_BLOCK, win), jnp.bfloat16),
            pltpu.VMEM((2, Q_BLOCKS_PER_STEP, Q_BLOCK, LANES), jnp.float32),
        ],
        compiler_params=pltpu.CompilerParams(
            dimension_semantics=("arbitrary", "arbitrary"),
            vmem_limit_bytes=VMEM_LIMIT_BYTES),
        name="dil_attn",
    )(q, q, k, v, v)


def _tail_kernel(x_ref, p_ref, attn_ref, sga_ref, conv_ref, na_ref, wout_ref, nple_ref,
                 wgate_ref, bgate_ref, wple_ref, nfin_ref, y_ref):
    f32 = jnp.float32
    attn = attn_ref[...].astype(f32)
    gated_attn = attn * _rms_scale(attn) * na_ref[...] * sga_ref[...].astype(f32)
    merged = jnp.concatenate([gated_attn.astype(jnp.bfloat16), conv_ref[...]], axis=1)
    x1 = x_ref[...] + jnp.dot(merged, wout_ref[...], preferred_element_type=f32)

    r = (x1 * _rms_scale(x1) * nple_ref[...]).astype(jnp.bfloat16)
    gate = jax.nn.sigmoid(jnp.dot(r, wgate_ref[...], preferred_element_type=f32)
                          + bgate_ref[...])
    emb = jnp.dot(p_ref[...].astype(jnp.bfloat16), wple_ref[...],
                  preferred_element_type=f32)
    x2 = x1 + emb * gate
    y_ref[...] = x2 * _rms_scale(x2) * nfin_ref[...]


def _tail(x2, p2, attn, sga, conv, norm_attn, w_out, norm_ple, w_gate, b_gate, w_ple,
          norm_final, tm):
    T = x2.shape[0]
    row = lambda i: (i, 0)
    fixed = lambda i: (0, 0)
    a_spec = pl.BlockSpec((tm, ATTN_WIDTH), row)
    full = lambda a: pl.BlockSpec(a.shape, fixed)
    return pl.pallas_call(
        _tail_kernel,
        grid=(T // tm,),
        in_specs=[
            pl.BlockSpec((tm, D_MODEL), row),
            pl.BlockSpec((tm, PLE_DIM), row),
            a_spec, a_spec, a_spec,
            full(norm_attn), full(w_out), full(norm_ple), full(w_gate), full(b_gate),
            full(w_ple), full(norm_final),
        ],
        out_specs=pl.BlockSpec((tm, D_MODEL), row),
        out_shape=jax.ShapeDtypeStruct((T, D_MODEL), jnp.float32),
        compiler_params=pltpu.CompilerParams(
            dimension_semantics=("arbitrary",),
            vmem_limit_bytes=VMEM_LIMIT_BYTES),
        name="tail",
    )(x2, p2, attn, sga, conv, norm_attn, w_out, norm_ple, w_gate, b_gate, w_ple, norm_final)


def _rope_tables(seq):
    half = ROT_DIM // 2
    inv = jnp.power(jnp.float32(ROPE_THETA),
                    -jnp.arange(0, ROT_DIM, 2, dtype=jnp.float32) / ROT_DIM)
    ang = jnp.arange(seq, dtype=jnp.float32)[:, None] * inv[None, :]
    cos, sin = jnp.cos(ang), jnp.sin(ang)
    pad = HEAD_DIM - ROT_DIM
    ones = jnp.ones((seq, pad), jnp.float32)
    zeros = jnp.zeros((seq, pad), jnp.float32)
    zh = jnp.zeros((seq, half), jnp.float32)
    per_head = lambda parts: jnp.tile(jnp.concatenate(parts, axis=1), (1, LANES // HEAD_DIM))
    return (per_head([cos, cos, ones]),
            per_head([-sin, zh, zeros]),
            per_head([zh, sin, zeros]))


def _trunk(x, p, tables, norm_mix, w_in, conv_w, norm_attn, norm_conv, w_out,
           norm_ple, w_gate, b_gate, w_ple, norm_final):
    B, S, _ = x.shape
    assert S % (16 * Q_BLOCK) == 0
    x2 = x.reshape(B * S, D_MODEL)
    p2 = p.reshape(B * S, PLE_DIM)
    cos_t, sinlo_t, sinhi_t = tables
    q, k, v, sga, conv = _in_proj(x2, norm_mix, w_in, cos_t[:S], sinlo_t[:S], sinhi_t[:S],
                                  conv_w, norm_conv, S, tm=1024)
    attn = _dil_attn(q, k, v, B, S)
    y = _tail(x2, p2, attn, sga, conv, norm_attn, w_out, norm_ple, w_gate, b_gate, w_ple,
              norm_final, tm=1024)
    return y.reshape(B, S, D_MODEL)


def kernel(x_prompt, x_sample, p_prompt, p_sample, norm_mix, w_in, conv_w,
           norm_attn_out, norm_conv_out, w_out, norm_ple, w_ple_gate, b_ple_gate,
           w_ple_proj, norm_final):
    depth = w_in.shape[0]
    assert depth == 1, "single-layer trunk"
    bf16 = jnp.bfloat16
    tables = _rope_tables(max(x_prompt.shape[1], x_sample.shape[1]))
    params = (norm_mix[0][None], w_in[0].astype(bf16), conv_w[0],
              norm_attn_out[0][None], norm_conv_out[0][None], w_out[0].astype(bf16),
              norm_ple[0][None], w_ple_gate[0].astype(bf16), b_ple_gate[0][None],
              w_ple_proj[0].astype(bf16), norm_final[None])
    y_prompt = _trunk(x_prompt, p_prompt[0], tables, *params)
    y_sample = _trunk(x_sample, p_sample[0], tables, *params)
    return (y_prompt, y_sample)
```

```python
import functools

import jax
import jax.numpy as jnp
from jax import lax
from jax.experimental import pallas as pl
from jax.experimental.pallas import tpu as pltpu

D_MODEL = 1024
ATTN_WIDTH = 512
CONV_WIDTH = 512
HEAD_DIM = 64
ROT_DIM = 16
ROPE_THETA = 500000.0
HALF_BAND = 64
PLE_DIM = 256
EPS = 1e-6
NEG = -1e30
LOG2_E = 1.4426950408889634

LANES = 128
F32_SUBLANES = 8
BF16_SUBLANES = 16
Q_BLOCK = 128
Q_BLOCKS_PER_STEP = 4
COPY_ROWS = 128
MERGE_UNROLL = 8
VMEM_LIMIT_BYTES = 56 * 1024 * 1024

_TRANS_B = (((1,), (1,)), ((), ()))


def _rms_scale(v):
    return lax.rsqrt(jnp.mean(v * v, axis=-1, keepdims=True) + EPS)


def _in_proj_kernel(x_ref, xprev_ref, xnext_ref, g_ref, w_ref, cos_ref, sinlo_ref, sinhi_ref,
                    convw_ref, nconv_ref, q_ref, k_ref, v_ref, sga_ref, conv_ref,
                    *, tiles_per_seq):
    tm = x_ref.shape[0]

    def normed(x):
        return (x * _rms_scale(x) * g_ref[...]).astype(jnp.bfloat16)

    h = normed(x_ref[...])

    def cols(j, hh=None, n=1):
        return jnp.dot(h if hh is None else hh,
                       w_ref[:, j * ATTN_WIDTH:(j + n) * ATTN_WIDTH],
                       preferred_element_type=jnp.float32)

    cos = cos_ref[...]
    sinlo = sinlo_ref[...]
    sinhi = sinhi_ref[...]

    def rope(t):
        outs = []
        for c in range(ATTN_WIDTH // LANES):
            tc = t[:, c * LANES:(c + 1) * LANES]
            up = pltpu.roll(tc, LANES - ROT_DIM // 2, axis=1)
            dn = pltpu.roll(tc, ROT_DIM // 2, axis=1)
            outs.append(tc * cos + up * sinlo + dn * sinhi)
        return jnp.concatenate(outs, axis=1)

    first_head = lax.broadcasted_iota(jnp.int32, (1, LANES), 1) < HEAD_DIM

    def per_head(t, fill):
        outs = []
        for c in range(ATTN_WIDTH // LANES):
            tc = t[:, c * LANES:(c + 1) * LANES]
            outs += [jnp.where(first_head, tc, fill), jnp.where(first_head, fill, tc)]
        return jnp.concatenate(outs, axis=1).astype(jnp.bfloat16)

    def words(t):
        return pltpu.bitcast(t, jnp.uint32)

    halo = normed(jnp.concatenate([xprev_ref[...], xnext_ref[...]], axis=0))
    cc_ch = cols(5, jnp.concatenate([h, halo], axis=0), 2)
    u_ext = cc_ch[:, :CONV_WIDTH] * cc_ch[:, CONV_WIDTH:]
    u = u_ext[:tm]
    t_in_seq = pl.program_id(0) % tiles_per_seq
    prev_row = u_ext[tm + F32_SUBLANES - 1:tm + F32_SUBLANES, :]
    next_row = u_ext[tm + F32_SUBLANES:tm + F32_SUBLANES + 1, :]
    prev_row = jnp.where(t_in_seq == 0, 0.0, prev_row)
    next_row = jnp.where(t_in_seq == tiles_per_seq - 1, 0.0, next_row)

    q_ref[...] = words(per_head(rope(cols(0)) * (HEAD_DIM ** -0.5 * LOG2_E), 0.0))

    ridx = lax.broadcasted_iota(jnp.int32, (tm, 1), 0)
    u_before = jnp.where(ridx == 0, prev_row, pltpu.roll(u, 1, axis=0))
    u_after = jnp.where(ridx == tm - 1, next_row, pltpu.roll(u, tm - 1, axis=0))
    cw = convw_ref[...]
    taps = cw[0:1] * u_before + cw[1:2] * u + cw[2:3] * u_after

    k_ref[...] = words(rope(cols(1)).astype(jnp.bfloat16))

    conv = cols(4) * taps
    conv = conv * _rms_scale(conv) * nconv_ref[...]

    v_ref[...] = words(per_head(cols(2), 1.0))

    conv_ref[...] = (conv * jax.nn.silu(cols(7))).astype(jnp.bfloat16)
    sga_ref[...] = jax.nn.silu(cols(3)).astype(jnp.bfloat16)


def _in_proj(x2, g, w_bf16, cos_t, sinlo_t, sinhi_t, conv_w, norm_conv, seq, tm):
    T = x2.shape[0]
    nseq = seq // tm
    halo_blocks = T // F32_SUBLANES
    per_tile = tm // F32_SUBLANES
    row = lambda i: (i, 0)
    fixed = lambda i: (0, 0)
    pos = lambda i: (i % nseq, 0)
    prev = lambda i: (jnp.maximum(i * per_tile - 1, 0), 0)
    nxt = lambda i: (jnp.minimum((i + 1) * per_tile, halo_blocks - 1), 0)
    out = jax.ShapeDtypeStruct((T, ATTN_WIDTH), jnp.bfloat16)
    wout = jax.ShapeDtypeStruct((T // 2, ATTN_WIDTH), jnp.uint32)
    wout2 = jax.ShapeDtypeStruct((T // 2, 2 * ATTN_WIDTH), jnp.uint32)
    spec = pl.BlockSpec((tm, ATTN_WIDTH), row)
    wspec = pl.BlockSpec((tm // 2, ATTN_WIDTH), row)
    wspec2 = pl.BlockSpec((tm // 2, 2 * ATTN_WIDTH), row)
    full = lambda a: pl.BlockSpec(a.shape, fixed)
    return pl.pallas_call(
        functools.partial(_in_proj_kernel, tiles_per_seq=nseq),
        grid=(T // tm,),
        in_specs=[
            pl.BlockSpec((tm, D_MODEL), row),
            pl.BlockSpec((F32_SUBLANES, D_MODEL), prev),
            pl.BlockSpec((F32_SUBLANES, D_MODEL), nxt),
            full(g), full(w_bf16),
            pl.BlockSpec((tm, LANES), pos),
            pl.BlockSpec((tm, LANES), pos),
            pl.BlockSpec((tm, LANES), pos),
            full(conv_w), full(norm_conv),
        ],
        out_specs=[wspec2, wspec, wspec2, spec, spec],
        out_shape=[wout2, wout, wout2, out, out],
        compiler_params=pltpu.CompilerParams(
            dimension_semantics=("arbitrary",),
            vmem_limit_bytes=VMEM_LIMIT_BYTES),
        name="in_proj",
    )(x2, x2, x2, g, w_bf16, cos_t, sinlo_t, sinhi_t, conv_w, norm_conv)


def _dil_attn_kernel(qa_ref, qb_ref, k_ref, va_ref, vb_ref, out_ref, fbuf, a16, b16, bias_scr,
                     s_scr, p_scr, m_scr, *, seq):
    f32, bf16, u32 = jnp.float32, jnp.bfloat16, jnp.uint32
    len4, len16 = seq // 4, seq // 16
    first_head = lax.broadcasted_iota(jnp.int32, (1, LANES), 1) < HEAD_DIM
    inputs = (qa_ref, qb_ref, k_ref, va_ref, vb_ref)
    n_copy = seq // (2 * COPY_ROWS)

    def rows_bf16(ref, lead, start, size):
        words = pl.ds(pl.multiple_of(start // 2, BF16_SUBLANES // 2), size // 2)
        return pltpu.bitcast(ref[(*lead, words, slice(None))], bf16)

    def split_row_pairs(words, lead, word0, dst, x, row_even, row_odd):
        w = words[(*lead, pl.ds(word0, COPY_ROWS, stride=2), slice(None))]
        even, odd = (pltpu.unpack_elementwise(w, index=i, packed_dtype=bf16,
                                              unpacked_dtype=f32).astype(bf16) for i in (0, 1))
        half = COPY_ROWS // 2
        dst[x, pl.ds(pl.multiple_of(row_even // 2, half), half), :] = pltpu.bitcast(even, u32)
        dst[x, pl.ds(pl.multiple_of(row_odd // 2, half), half), :] = pltpu.bitcast(odd, u32)

    def split4(c):
        chunks = len4 // COPY_ROWS
        pair = c // chunks
        u0 = (c % chunks) * COPY_ROWS
        for x in range(5):
            split_row_pairs(inputs[x], (), pair + 2 * u0, a16, x,
                            2 * pair * len4 + u0, (2 * pair + 1) * len4 + u0)

    def split16(c):
        chunks = len16 // COPY_ROWS
        w0 = (c % chunks) * COPY_ROWS
        pair = (c // chunks) % 2
        r4 = c // (2 * chunks)
        for x in range(5):
            split_row_pairs(a16, (x,), r4 * (len4 // 2) + pair + 2 * w0,
                            b16, x, (r4 + 8 * pair) * len16 + w0,
                            (r4 + 8 * pair + 4) * len16 + w0)

    win_max = Q_BLOCK + 2 * HALF_BAND
    rel = (lax.broadcasted_iota(jnp.int32, (Q_BLOCK, win_max), 0)
           - lax.broadcasted_iota(jnp.int32, (Q_BLOCK, win_max), 1))
    for n in range(3):
        bias_scr[n] = jnp.where(jnp.abs(rel + n * HALF_BAND) <= HALF_BAND, 0.0, NEG)

    def run_group(src, dil, out_slot, relayout=None):
        cls_len = seq // dil
        win = min(cls_len, win_max)
        blocks_per_class = cls_len // Q_BLOCK

        n_steps = seq // (Q_BLOCK * Q_BLOCKS_PER_STEP)

        def blocks(step):
            for jj in range(Q_BLOCKS_PER_STEP):
                t = step * Q_BLOCKS_PER_STEP + jj
                row = pl.multiple_of(t * Q_BLOCK, Q_BLOCK)
                cls = t // blocks_per_class
                q0 = (t % blocks_per_class) * Q_BLOCK
                w0 = jnp.clip(q0 - HALF_BAND, 0, cls_len - win)
                start = pl.multiple_of(cls * cls_len + w0, BF16_SUBLANES)
                yield jj, row, cls, q0, w0, start

        def scores(step, par):
            for jj, row, cls, q0, w0, start in blocks(step):
                bias = bias_scr[(q0 - w0) // HALF_BAND, :, :win]
                q_ab = jnp.concatenate([src(0, row, Q_BLOCK), src(1, row, Q_BLOCK)], axis=0)
                s = lax.dot_general(q_ab, src(2, start, win), _TRANS_B,
                                    preferred_element_type=f32)
                s_scr[par, 2 * jj, :, :win] = s[:Q_BLOCK] + bias
                s_scr[par, 2 * jj + 1, :, :win] = s[Q_BLOCK:] + bias

        def softmax(par):
            for jj in range(Q_BLOCKS_PER_STEP):
                maxes = []
                for idx in (2 * jj, 2 * jj + 1):
                    s = s_scr[par, idx, :, :win]
                    m = jnp.max(s, axis=-1, keepdims=True)
                    p_scr[par, idx, :, :win] = jnp.exp2(s - m).astype(bf16)
                    maxes.append(m)
                m_scr[par, jj] = jnp.where(first_head, maxes[0], maxes[1])

        def values(step, par):
            for jj, row, cls, q0, w0, start in blocks(step):
                r_a = jnp.dot(p_scr[par, 2 * jj, :, :win], src(3, start, win),
                              preferred_element_type=f32)
                r_b = jnp.dot(p_scr[par, 2 * jj + 1, :, :win], src(4, start, win),
                              preferred_element_type=f32)
                o_un = jnp.where(first_head, r_a, r_b)
                den = pltpu.roll(jnp.where(first_head, r_b, r_a), HEAD_DIM, axis=1)
                if dil == 16:
                    rows = pl.ds((cls % 4) * len4 + cls // 4 + 4 * q0, Q_BLOCK, stride=4)
                else:
                    rows = pl.ds(row, Q_BLOCK)
                fbuf[out_slot, rows, :] = o_un / den
                fbuf[out_slot + 1, rows, :] = m_scr[par, jj] + jnp.log2(den)

        assert n_steps % 2 == 0 and n_steps >= 2
        n_loop = (n_steps - 2) // 2
        per_part = n_copy // (n_loop + 2) if relayout else 0

        def fill(first, count):
            for i in range(count):
                relayout(first + i)

        scores(0, 0)
        softmax(0)
        scores(1, 1)
        fill(0, per_part)

        def body(j, carry):
            step = 2 * j
            values(step, 0)
            softmax(1)
            scores(step + 2, 0)
            fill((j + 1) * per_part, per_part)
            values(step + 1, 1)
            softmax(0)
            scores(step + 3, 1)
            return carry

        lax.fori_loop(0, n_loop, body, 0)
        values(n_steps - 2, 0)
        softmax(1)
        if relayout:
            fill((n_loop + 1) * per_part, n_copy - (n_loop + 1) * per_part)
        values(n_steps - 1, 1)

    run_group(lambda n, start, size: rows_bf16(inputs[n], (), start, size), 1, 4, split4)
    run_group(lambda n, start, size: rows_bf16(a16, (n,), start, size), 4, 2, split16)
    run_group(lambda n, start, size: rows_bf16(b16, (n,), start, size), 16, 0)

    def merged_class(r4, u0):
        by_class = pl.ds(pl.multiple_of(r4 * len4 + u0, COPY_ROWS), COPY_ROWS)
        by_token = pl.ds(r4 + 4 * u0, COPY_ROWS, stride=4)
        rows = (by_class, by_class, by_token)
        lse = [fbuf[2 * g + 1, rows[g], :] for g in range(3)]
        top = jnp.maximum(jnp.maximum(lse[0], lse[1]), lse[2])
        w = [jnp.exp2(l - top) for l in lse]
        num = sum(w[g] * fbuf[2 * g, rows[g], :] for g in range(3))
        return num / (w[0] + w[1] + w[2])

    def merge(c, carry):
        chunks = len4 // COPY_ROWS
        pair = c // chunks
        u0 = (c % chunks) * COPY_ROWS
        halves = [merged_class(2 * pair + i, u0) for i in range(2)]
        out_ref[pl.ds(pair + 2 * u0, COPY_ROWS, stride=2), :] = pltpu.pack_elementwise(
            halves, packed_dtype=bf16)
        return carry

    lax.fori_loop(0, seq // (2 * COPY_ROWS), merge, 0, unroll=MERGE_UNROLL // 2)


def _dil_attn(q, k, v, batch, seq):
    words = seq // 2
    pair = pl.BlockSpec((words, LANES), lambda b, c: (b, c))
    head_a = pl.BlockSpec((words, LANES), lambda b, c: (b, 2 * c))
    head_b = pl.BlockSpec((words, LANES), lambda b, c: (b, 2 * c + 1))
    n_tiles = 2 * Q_BLOCKS_PER_STEP
    win = Q_BLOCK + 2 * HALF_BAND
    return pl.pallas_call(
        functools.partial(_dil_attn_kernel, seq=seq),
        grid=(batch, ATTN_WIDTH // LANES),
        in_specs=[head_a, head_b, pair, head_a, head_b],
        out_specs=pair,
        out_shape=jax.ShapeDtypeStruct((batch * words, ATTN_WIDTH), jnp.uint32),
        scratch_shapes=[
            pltpu.VMEM((6, seq, LANES), jnp.float32),
            pltpu.VMEM((5, words, LANES), jnp.uint32),
            pltpu.VMEM((5, words, LANES), jnp.uint32),
            pltpu.VMEM((3, Q_BLOCK, win), jnp.float32),
            pltpu.VMEM((2, n_tiles, Q_BLOCK, win), jnp.float32),
            pltpu.VMEM((2, n_tiles, Q_BLOCK, win), jnp.bfloat16),
            pltpu.VMEM((2, Q_BLOCKS_PER_STEP, Q_BLOCK, LANES), jnp.float32),
        ],
        compiler_params=pltpu.CompilerParams(
            dimension_semantics=("arbitrary", "arbitrary"),
            vmem_limit_bytes=VMEM_LIMIT_BYTES),
        name="dil_attn",
    )(q, q, k, v, v)


def _tail_kernel(x_ref, p_ref, attn_ref, sga_ref, conv_ref, na_ref, wout_ref, nple_ref,
                 wgate_ref, bgate_ref, wple_ref, nfin_ref, y_ref):
    f32 = jnp.float32
    attn = pltpu.bitcast(attn_ref[...], jnp.bfloat16).astype(f32)
    gated_attn = attn * _rms_scale(attn) * na_ref[...] * sga_ref[...].astype(f32)
    merged = jnp.concatenate([gated_attn.astype(jnp.bfloat16), conv_ref[...]], axis=1)
    x1 = x_ref[...] + jnp.dot(merged, wout_ref[...], preferred_element_type=f32)

    r = (x1 * _rms_scale(x1) * nple_ref[...]).astype(jnp.bfloat16)
    gate = jax.nn.sigmoid(jnp.dot(r, wgate_ref[...], preferred_element_type=f32)
                          + bgate_ref[...])
    emb = jnp.dot(p_ref[...].astype(jnp.bfloat16), wple_ref[...],
                  preferred_element_type=f32)
    x2 = x1 + emb * gate
    y_ref[...] = x2 * _rms_scale(x2) * nfin_ref[...]


def _tail(x2, p2, attn, sga, conv, norm_attn, w_out, norm_ple, w_gate, b_gate, w_ple,
          norm_final, tm):
    T = x2.shape[0]
    row = lambda i: (i, 0)
    fixed = lambda i: (0, 0)
    a_spec = pl.BlockSpec((tm, ATTN_WIDTH), row)
    full = lambda a: pl.BlockSpec(a.shape, fixed)
    return pl.pallas_call(
        _tail_kernel,
        grid=(T // tm,),
        in_specs=[
            pl.BlockSpec((tm, D_MODEL), row),
            pl.BlockSpec((tm, PLE_DIM), row),
            pl.BlockSpec((tm // 2, ATTN_WIDTH), row), a_spec, a_spec,
            full(norm_attn), full(w_out), full(norm_ple), full(w_gate), full(b_gate),
            full(w_ple), full(norm_final),
        ],
        out_specs=pl.BlockSpec((tm, D_MODEL), row),
        out_shape=jax.ShapeDtypeStruct((T, D_MODEL), jnp.float32),
        compiler_params=pltpu.CompilerParams(
            dimension_semantics=("arbitrary",),
            vmem_limit_bytes=VMEM_LIMIT_BYTES),
        name="tail",
    )(x2, p2, attn, sga, conv, norm_attn, w_out, norm_ple, w_gate, b_gate, w_ple, norm_final)


def _rope_tables(seq):
    half = ROT_DIM // 2
    inv = jnp.power(jnp.float32(ROPE_THETA),
                    -jnp.arange(0, ROT_DIM, 2, dtype=jnp.float32) / ROT_DIM)
    ang = jnp.arange(seq, dtype=jnp.float32)[:, None] * inv[None, :]
    cos, sin = jnp.cos(ang), jnp.sin(ang)
    pad = HEAD_DIM - ROT_DIM
    ones = jnp.ones((seq, pad), jnp.float32)
    zeros = jnp.zeros((seq, pad), jnp.float32)
    zh = jnp.zeros((seq, half), jnp.float32)
    per_head = lambda parts: jnp.tile(jnp.concatenate(parts, axis=1), (1, LANES // HEAD_DIM))
    return (per_head([cos, cos, ones]),
            per_head([-sin, zh, zeros]),
            per_head([zh, sin, zeros]))


def _trunk(x, p, tables, norm_mix, w_in, conv_w, norm_attn, norm_conv, w_out,
           norm_ple, w_gate, b_gate, w_ple, norm_final):
    B, S, _ = x.shape
    assert S % (16 * Q_BLOCK) == 0
    x2 = x.reshape(B * S, D_MODEL)
    p2 = p.reshape(B * S, PLE_DIM)
    cos_t, sinlo_t, sinhi_t = tables
    q, k, v, sga, conv = _in_proj(x2, norm_mix, w_in, cos_t[:S], sinlo_t[:S], sinhi_t[:S],
                                  conv_w, norm_conv, S, tm=1024)
    attn = _dil_attn(q, k, v, B, S)
    y = _tail(x2, p2, attn, sga, conv, norm_attn, w_out, norm_ple, w_gate, b_gate, w_ple,
              norm_final, tm=1024)
    return y.reshape(B, S, D_MODEL)


def kernel(x_prompt, x_sample, p_prompt, p_sample, norm_mix, w_in, conv_w,
           norm_attn_out, norm_conv_out, w_out, norm_ple, w_ple_gate, b_ple_gate,
           w_ple_proj, norm_final):
    depth = w_in.shape[0]
    assert depth == 1, "single-layer trunk"
    bf16 = jnp.bfloat16
    tables = _rope_tables(max(x_prompt.shape[1], x_sample.shape[1]))
    params = (norm_mix[0][None], w_in[0].astype(bf16), conv_w[0],
              norm_attn_out[0][None], norm_conv_out[0][None], w_out[0].astype(bf16),
              norm_ple[0][None], w_ple_gate[0].astype(bf16), b_ple_gate[0][None],
              w_ple_proj[0].astype(bf16), norm_final[None])
    y_prompt = _trunk(x_prompt, p_prompt[0], tables, *params)
    y_sample = _trunk(x_sample, p_sample[0], tables, *params)
    return (y_prompt, y_sample)
```
